```python
import jax, jax.numpy as jnp
from jax import lax
import numpy as np

D_MODEL = 2048
BATCH = 4
SEQ = 2048
DEPTH = 4
DEC_BATCH = 128
DEC_SEQ = 8
PAST_LEN = 16384
PAGE_SIZE = 128

N_EVEN = (DEPTH + 1) // 2
N_ODD = DEPTH // 2
PLE_DIM = 256
NORM_EPS = 1e-6
D_A = D_MODEL // 2
CHUNK = 128
A_GROUP_DIM = 128
A_GROUPS = D_A // A_GROUP_DIM
LN_EPS = 1e-5
D_B = D_MODEL // 2
CONV_W = 3
D_IN_EVEN = 2 * D_A + 3 * D_B
RW_HEAD_DIM = 64
RW_HEADS = D_MODEL // RW_HEAD_DIM
RW_DECAY_LORA = 96
RW_AAA_LORA = 96
RW_GATE_LORA = 256
RW_GN_EPS = 64e-5
N_GROUPS = 4
EXP_PER_GROUP = 4
N_EXPERTS = N_GROUPS * EXP_PER_GROUP
TOP_K = 2
D_EXPERT = 512

kernel_name = 'hybrid_gmlp_shortconv_rwkv7_hmoe_step'


def rmsnorm(x, g):
    xf = x.astype(jnp.float32)
    y = xf * lax.rsqrt(jnp.mean(xf * xf, axis=-1, keepdims=True) + NORM_EPS)
    return (y * g.astype(jnp.float32)).astype(x.dtype)


def layernorm(x, g, b, eps):
    xf = x.astype(jnp.float32)
    mu = jnp.mean(xf, axis=-1, keepdims=True)
    var = jnp.mean(jnp.square(xf - mu), axis=-1, keepdims=True)
    return ((xf - mu) * lax.rsqrt(var + eps) * g.astype(jnp.float32) + b.astype(jnp.float32)).astype(x.dtype)


def chunk_spatial_gate(u, v, w_s, b_s):
    bsz, L, _ = v.shape
    Lc = min(L, CHUNK)
    n = -(-L // Lc)
    pad = n * Lc - L
    vp = jnp.pad(v, ((0, 0), (0, pad), (0, 0))).reshape(bsz, n, Lc, A_GROUPS, A_GROUP_DIM)
    causal = jnp.tril(jnp.ones((Lc, Lc), dtype=bool))
    w = jnp.where(causal[None], w_s[:, :Lc, :Lc], 0.0).astype(v.dtype)
    mixed = jnp.einsum('gts,bnsgc->bntgc', w, vp) + b_s[:, :Lc].T[None, None, :, :, None].astype(v.dtype)
    mixed = mixed.reshape(bsz, n * Lc, D_A)[:, :L]
    return u * mixed


def even_mixer(h, conv_buf, w_in, vn_g, vn_b, w_s, b_s, conv_w, w_out):
    L = h.shape[1]
    proj = h @ w_in
    u, v, x_in, gate_b, gate_c = jnp.split(proj, [D_A, 2 * D_A, 2 * D_A + D_B, 2 * D_A + 2 * D_B], axis=-1)
    u = jax.nn.gelu(u)
    v = layernorm(jax.nn.gelu(v), vn_g, vn_b, LN_EPS)
    y_a = chunk_spatial_gate(u, v, w_s, b_s)
    cx = gate_c * x_in
    full = jnp.concatenate([conv_buf.astype(cx.dtype), cx], axis=1)
    conv = full[:, 0:L] * conv_w[0]
    for k in range(1, CONV_W):
        conv = conv + full[:, k:k + L] * conv_w[k]
    y_b = gate_b * conv
    out = jnp.concatenate([y_a, y_b], axis=-1) @ w_out
    start = ((L - 1) // CHUNK) * CHUNK
    return out, v[:, start:], full[:, L:]


def rwkv7_mixer(h, shift_prev, s0, mu, w_rkv, w0, w1, w2, a0, a1, a2, g1, g2, k_k, k_a, r_k, gn_g, gn_b, w_o):
    f32 = jnp.float32
    bsz, L, D = h.shape
    prev = jnp.concatenate([shift_prev[:, None, :].astype(h.dtype), h[:, :-1]], axis=1)
    xx = prev - h
    xr, xw, xk, xv, xa, xg = [h + xx * mu[n] for n in range(6)]
    rkv = jnp.einsum('nbld,nde->nble', jnp.stack([xr, xk, xv]), w_rkv)
    r, k, v = rkv[0], rkv[1], rkv[2]
    w_log = -jax.nn.softplus(-(w0 + jnp.tanh(xw @ w1) @ w2).astype(f32)) - 0.5
    decay = jnp.exp(-jnp.exp(w_log))
    a = jax.nn.sigmoid((a0 + (xa @ a1) @ a2).astype(f32))
    g = jax.nn.sigmoid(xg @ g1) @ g2
    kk = (k * k_k).astype(f32).reshape(bsz, L, RW_HEADS, RW_HEAD_DIM)
    kk = kk / jnp.maximum(jnp.sqrt(jnp.sum(kk * kk, axis=-1, keepdims=True)), 1e-12)
    k_mod = k.astype(f32) * (1.0 + (a - 1.0) * k_a.astype(f32))

    def heads(t):
        return t.astype(f32).reshape(bsz, L, RW_HEADS, RW_HEAD_DIM)

    r_h, w_h, k_h, v_h, a_h = heads(r), heads(decay), heads(k_mod), heads(v), heads(a)

    def step(S, inp):
        r_t, w_t, k_t, v_t, kk_t, a_t = inp
        sa = jnp.einsum('bhij,bhj->bhi', S, kk_t)
        S = S * w_t[:, :, None, :] - sa[..., None] * (kk_t * a_t)[:, :, None, :] + v_t[..., None] * k_t[:, :, None, :]
        y = jnp.einsum('bhij,bhj->bhi', S, r_t)
        return S, y

    xs = tuple(jnp.moveaxis(t, 1, 0) for t in (r_h, w_h, k_h, v_h, kk, a_h))
    s_T, ys = lax.scan(step, s0.astype(f32), xs)
    ys = jnp.moveaxis(ys, 0, 1)
    mu_y = jnp.mean(ys, axis=-1, keepdims=True)
    var_y = jnp.mean(jnp.square(ys - mu_y), axis=-1, keepdims=True)
    yn = ((ys - mu_y) * lax.rsqrt(var_y + RW_GN_EPS)).reshape(bsz, L, D) * gn_g.astype(f32) + gn_b.astype(f32)
    bonus = (jnp.sum(r_h * k_h * r_k.astype(f32), axis=-1, keepdims=True) * v_h).reshape(bsz, L, D)
    out = ((yn + bonus).astype(h.dtype) * g) @ w_o
    return out, h[:, -1], s_T.astype(s0.dtype)


def hier_moe(h, w_gr, b_gr, w_er, b_er, w_gate, w_up, w_down):
    f32 = jnp.float32
    bsz, L, D = h.shape
    t = h.reshape(-1, D)
    T = t.shape[0]
    tf = t.astype(f32)
    rows = jnp.arange(T)
    g_logits = tf @ w_gr.astype(f32) + b_gr.astype(f32)
    g_prob = jax.nn.softmax(g_logits, axis=-1)
    g_idx = jnp.argmax(g_logits, axis=-1)
    p_group = g_prob[rows, g_idx][:, None]
    e_logits = (tf @ w_er.astype(f32) + b_er.astype(f32)).reshape(T, N_GROUPS, EXP_PER_GROUP)
    e_prob = jax.nn.softmax(e_logits[rows, g_idx], axis=-1)
    top_p, top_i = lax.top_k(e_prob, TOP_K)
    top_p = top_p / jnp.sum(top_p, axis=-1, keepdims=True)
    expert_id = g_idx[:, None] * EXP_PER_GROUP + top_i
    gates = jnp.sum(jax.nn.one_hot(expert_id, N_EXPERTS, dtype=f32) * (p_group * top_p)[..., None], axis=1)
    hg = jnp.einsum('td,edf->tef', t, w_gate)
    hu = jnp.einsum('td,edf->tef', t, w_up)
    act = jax.nn.silu(hg) * hu * gates[..., None].astype(t.dtype)
    out = jnp.einsum('tef,efd->td', act, w_down)
    return out.reshape(bsz, L, D)


def per_layer_embed(x, p_i, g_ple, w_pg, w_pp):
    gate = jax.nn.sigmoid((rmsnorm(x, g_ple) @ w_pg).astype(jnp.float32)).astype(x.dtype)
    return x + gate * (p_i.astype(x.dtype) @ w_pp)


def run_trunk(x, p, conv0, shift0, wkv0, P):
    chunk_v, conv_new, shift_new, wkv_new = [], [], [], []
    for i in range(DEPTH):
        j = i // 2
        h = rmsnorm(x, P['g_mix'][i])
        if i % 2 == 0:
            out, v_rows, buf = even_mixer(h, conv0[j], P['w_in_even'][j], P['vn_g'][j], P['vn_b'][j],
                                          P['w_s'][j], P['b_s'][j], P['conv_w'][j], P['w_out_even'][j])
            chunk_v.append(v_rows)
            conv_new.append(buf)
        else:
            out, last, s_new = rwkv7_mixer(h, shift0[j], wkv0[j], P['rw_mu'][j], P['rw_w_rkv'][j],
                                           P['rw_w0'][j], P['rw_w1'][j], P['rw_w2'][j],
                                           P['rw_a0'][j], P['rw_a1'][j], P['rw_a2'][j],
                                           P['rw_g1'][j], P['rw_g2'][j], P['rw_k_k'][j], P['rw_k_a'][j],
                                           P['rw_r_k'][j], P['rw_gn_g'][j], P['rw_gn_b'][j], P['rw_w_o'][j])
            shift_new.append(last)
            wkv_new.append(s_new)
        x = x + out
        x = x + hier_moe(rmsnorm(x, P['g_ffn'][i]), P['moe_w_gr'][i], P['moe_b_gr'][i], P['moe_w_er'][i],
                         P['moe_b_er'][i], P['moe_w_gate'][i], P['moe_w_up'][i], P['moe_w_down'][i])
        x = per_layer_embed(x, p[i], P['ple_g'][i], P['ple_w_gate'][i], P['ple_w_proj'][i])
    return rmsnorm(x, P['g_final']), jnp.stack(chunk_v), jnp.stack(conv_new), jnp.stack(shift_new), jnp.stack(wkv_new)


def setup_inputs(seed: int = 0) -> dict:
    key = jax.random.key(seed)
    ks = iter(jax.random.split(key, 64))
    f32 = jnp.float32
    D = D_MODEL

    def nrm(shape, scale):
        return jax.random.normal(next(ks), shape, f32) * scale

    def gain(shape):
        return 1.0 + nrm(shape, 0.05)

    inp = {}
    inp['x_prompt'] = nrm((BATCH, SEQ, D), 1.0)
    inp['x_sample'] = nrm((DEC_BATCH, DEC_SEQ, D), 1.0)
    inp['state_conv'] = nrm((N_EVEN, DEC_BATCH, CONV_W - 1, D_B), 1.0)
    inp['state_shift'] = nrm((N_ODD, DEC_BATCH, D), 1.0)
    inp['state_wkv'] = nrm((N_ODD, DEC_BATCH, RW_HEADS, RW_HEAD_DIM, RW_HEAD_DIM), 0.3)
    inp['p_prompt'] = nrm((DEPTH, BATCH, SEQ, PLE_DIM), 1.0)
    inp['p_sample'] = nrm((DEPTH, DEC_BATCH, DEC_SEQ, PLE_DIM), 1.0)
    inp['g_mix'] = gain((DEPTH, D))
    inp['g_ffn'] = gain((DEPTH, D))
    inp['g_final'] = gain((D,))
    inp['w_in_even'] = nrm((N_EVEN, D, D_IN_EVEN), D ** -0.5)
    inp['vn_g'] = gain((N_EVEN, D_A))
    inp['vn_b'] = nrm((N_EVEN, D_A), 0.02)
    inp['w_s'] = nrm((N_EVEN, A_GROUPS, CHUNK, CHUNK), CHUNK ** -0.5)
    inp['b_s'] = gain((N_EVEN, A_GROUPS, CHUNK))
    inp['conv_w'] = nrm((N_EVEN, CONV_W, D_B), CONV_W ** -0.5)
    inp['w_out_even'] = nrm((N_EVEN, D_A + D_B, D), (D_A + D_B) ** -0.5)
    inp['rw_mu'] = jax.random.uniform(next(ks), (N_ODD, 6, D), f32)
    inp['rw_w_rkv'] = nrm((N_ODD, 3, D, D), D ** -0.5)
    inp['rw_w0'] = jax.random.uniform(next(ks), (N_ODD, D), f32, -6.5, -1.5)
    inp['rw_w1'] = nrm((N_ODD, D, RW_DECAY_LORA), 0.1 * D ** -0.5)
    inp['rw_w2'] = nrm((N_ODD, RW_DECAY_LORA, D), RW_DECAY_LORA ** -0.5)
    inp['rw_a0'] = nrm((N_ODD, D), 0.3)
    inp['rw_a1'] = nrm((N_ODD, D, RW_AAA_LORA), D ** -0.5)
    inp['rw_a2'] = nrm((N_ODD, RW_AAA_LORA, D), 0.3 * RW_AAA_LORA ** -0.5)
    inp['rw_g1'] = nrm((N_ODD, D, RW_GATE_LORA), D ** -0.5)
    inp['rw_g2'] = nrm((N_ODD, RW_GATE_LORA, D), RW_GATE_LORA ** -0.5)
    inp['rw_k_k'] = 0.85 + nrm((N_ODD, D), 0.05)
    inp['rw_k_a'] = gain((N_ODD, D))
    inp['rw_r_k'] = nrm((N_ODD, RW_HEADS, RW_HEAD_DIM), 0.1)
    inp['rw_gn_g'] = gain((N_ODD, D))
    inp['rw_gn_b'] = nrm((N_ODD, D), 0.02)
    inp['rw_w_o'] = nrm((N_ODD, D, D), D ** -0.5)
    inp['moe_w_gr'] = nrm((DEPTH, D, N_GROUPS), D ** -0.5)
    inp['moe_b_gr'] = nrm((DEPTH, N_GROUPS), 0.01)
    inp['moe_w_er'] = nrm((DEPTH, D, N_EXPERTS), D ** -0.5)
    inp['moe_b_er'] = nrm((DEPTH, N_EXPERTS), 0.01)
    inp['moe_w_gate'] = nrm((DEPTH, N_EXPERTS, D, D_EXPERT), D ** -0.5)
    inp['moe_w_up'] = nrm((DEPTH, N_EXPERTS, D, D_EXPERT), D ** -0.5)
    inp['moe_w_down'] = nrm((DEPTH, N_EXPERTS, D_EXPERT, D), D_EXPERT ** -0.5)
    inp['ple_g'] = gain((DEPTH, D))
    inp['ple_w_gate'] = nrm((DEPTH, D, D), D ** -0.5)
    inp['ple_w_proj'] = nrm((DEPTH, PLE_DIM, D), PLE_DIM ** -0.5)
    return inp


def reference(x_prompt, x_sample, state_conv, state_shift, state_wkv, p_prompt, p_sample,
              g_mix, g_ffn, g_final,
              w_in_even, vn_g, vn_b, w_s, b_s, conv_w, w_out_even,
              rw_mu, rw_w_rkv, rw_w0, rw_w1, rw_w2, rw_a0, rw_a1, rw_a2, rw_g1, rw_g2,
              rw_k_k, rw_k_a, rw_r_k, rw_gn_g, rw_gn_b, rw_w_o,
              moe_w_gr, moe_b_gr, moe_w_er, moe_b_er, moe_w_gate, moe_w_up, moe_w_down,
              ple_g, ple_w_gate, ple_w_proj):
    P = dict(g_mix=g_mix, g_ffn=g_ffn, g_final=g_final,
             w_in_even=w_in_even, vn_g=vn_g, vn_b=vn_b, w_s=w_s, b_s=b_s, conv_w=conv_w, w_out_even=w_out_even,
             rw_mu=rw_mu, rw_w_rkv=rw_w_rkv, rw_w0=rw_w0, rw_w1=rw_w1, rw_w2=rw_w2,
             rw_a0=rw_a0, rw_a1=rw_a1, rw_a2=rw_a2, rw_g1=rw_g1, rw_g2=rw_g2,
             rw_k_k=rw_k_k, rw_k_a=rw_k_a, rw_r_k=rw_r_k, rw_gn_g=rw_gn_g, rw_gn_b=rw_gn_b, rw_w_o=rw_w_o,
             moe_w_gr=moe_w_gr, moe_b_gr=moe_b_gr, moe_w_er=moe_w_er, moe_b_er=moe_b_er,
             moe_w_gate=moe_w_gate, moe_w_up=moe_w_up, moe_w_down=moe_w_down,
             ple_g=ple_g, ple_w_gate=ple_w_gate, ple_w_proj=ple_w_proj)
    bp = x_prompt.shape[0]
    conv0 = jnp.zeros((N_EVEN, bp, CONV_W - 1, D_B), x_prompt.dtype)
    shift0 = jnp.zeros((N_ODD, bp, D_MODEL), x_prompt.dtype)
    wkv0 = jnp.zeros((N_ODD, bp, RW_HEADS, RW_HEAD_DIM, RW_HEAD_DIM), state_wkv.dtype)
    y_prompt, cv_p, conv_p, shift_p, wkv_p = run_trunk(x_prompt, p_prompt, conv0, shift0, wkv0, P)
    y_sample, cv_s, conv_s, shift_s, wkv_s = run_trunk(x_sample, p_sample, state_conv, state_shift, state_wkv, P)
    return (y_prompt, y_sample, cv_p, conv_p, shift_p, wkv_p, cv_s, conv_s, shift_s, wkv_s)
```

```python
import functools

import jax
import jax.numpy as jnp
from jax import lax
from jax.experimental import pallas as pl
from jax.experimental.pallas import tpu as pltpu

F32 = jnp.float32
BF16 = jnp.bfloat16

NORM_EPS = 1e-6
LN_EPS = 1e-5
RW_GN_EPS = 64e-5
CHUNK = 128
A_GROUP_DIM = 128
RW_HEAD_DIM = 64
N_GROUPS = 4
EXP_PER_GROUP = 4
N_EXPERTS = N_GROUPS * EXP_PER_GROUP

LANES = 128
SUBLANES = 8
VMEM_LIMIT = 52 * 1024 * 1024


def _cparams(*sem):
    return pltpu.CompilerParams(dimension_semantics=sem, vmem_limit_bytes=VMEM_LIMIT)


def _rms(x, g):
    return x * lax.rsqrt(jnp.mean(x * x, axis=-1, keepdims=True) + NORM_EPS) * g


def _gelu_tanh(x):
    return 0.5 * x * (1.0 + jnp.tanh(0.7978845608028654 * (x + 0.044715 * (x * x * x))))


def _sigmoid(x):
    return 1.0 / (1.0 + jnp.exp(-x))


def _dot(a, b):
    return jnp.dot(a, b, preferred_element_type=F32)


def _even_inproj_kernel(n_gelu, x_ref, g_ref, w_ref, o_ref, h_ref):
    n = pl.program_id(1)

    @pl.when(n == 0)
    def _():
        h_ref[...] = _rms(x_ref[...], g_ref[...]).astype(BF16)

    acc = _dot(h_ref[...], w_ref[...])

    @pl.when(n < n_gelu)
    def _():
        o_ref[...] = _gelu_tanh(acc)

    @pl.when(n >= n_gelu)
    def _():
        o_ref[...] = acc


def _even_inproj(x, g, w, d_a, tm, tn):
    T, D = x.shape
    N = w.shape[1]
    return pl.pallas_call(
        functools.partial(_even_inproj_kernel, (2 * d_a) // tn),
        grid=(T // tm, N // tn),
        in_specs=[pl.BlockSpec((tm, D), lambda i, n: (i, 0)),
                  pl.BlockSpec((1, D), lambda i, n: (0, 0)),
                  pl.BlockSpec((D, tn), lambda i, n: (0, n))],
        out_specs=pl.BlockSpec((tm, tn), lambda i, n: (i, n)),
        out_shape=jax.ShapeDtypeStruct((T, N), F32),
        scratch_shapes=[pltpu.VMEM((tm, D), BF16)],
        compiler_params=_cparams("arbitrary", "arbitrary"),
        name="even_inproj",
    )(x, g, w)


def _even_mix_kernel(prompt, seq_len, tiles_per_seq, tm, d_a,
                     u_ref, v_ref, xi_ref, gb_ref, gc_ref, x_ref, vng_ref, vnb_ref,
                     ws_ref, bias_ref, cw_ref, wout_ref, b1_ref, b2_ref,
                     xo_ref, vo_ref, cxo_ref, y_ref, carry_ref):
    i = pl.program_id(0)
    lc = min(seq_len, CHUNK)

    vg = v_ref[...]
    mu = jnp.mean(vg, axis=-1, keepdims=True)
    vc = vg - mu
    var = jnp.mean(vc * vc, axis=-1, keepdims=True)
    vln = vc * lax.rsqrt(var + LN_EPS) * vng_ref[...] + vnb_ref[...]
    vo_ref[...] = vln

    r = lax.broadcasted_iota(jnp.int32, (CHUNK, CHUNK), 0)
    c = lax.broadcasted_iota(jnp.int32, (CHUNK, CHUNK), 1)
    keep = (r >= c) & ((r // lc) == (c // lc))
    n_groups = d_a // A_GROUP_DIM
    for g in range(n_groups):
        wg = jnp.where(keep, ws_ref[g], 0.0).astype(BF16)
        cols = slice(g * A_GROUP_DIM, (g + 1) * A_GROUP_DIM)
        for ch in range(tm // CHUNK):
            rows = slice(ch * CHUNK, (ch + 1) * CHUNK)
            mixed = _dot(wg, vln[rows, cols].astype(BF16)) + bias_ref[:, cols]
            y_ref[rows, cols] = (u_ref[rows, cols] * mixed).astype(BF16)

    cx = gc_ref[...] * xi_ref[...]
    cxo_ref[...] = cx
    row = lax.broadcasted_iota(jnp.int32, cx.shape, 0)
    r1 = pltpu.roll(cx, 1, 0)
    r2 = pltpu.roll(cx, 2, 0)
    if prompt:
        @pl.when(i % tiles_per_seq == 0)
        def _():
            carry_ref[...] = jnp.zeros_like(carry_ref)
            carry_ref[SUBLANES - 2:SUBLANES, :] = b1_ref[0]
        p1 = carry_ref[SUBLANES - 1:SUBLANES, :]
        p2 = carry_ref[SUBLANES - 2:SUBLANES - 1, :]
        s1 = jnp.where(row == 0, p1, r1)
        s2 = jnp.where(row == 0, p2, jnp.where(row == 1, p1, r2))
        carry_ref[...] = cx[tm - SUBLANES:tm, :]
    else:
        pos = row % seq_len
        s1 = jnp.where(pos == 0, b1_ref[...], r1)
        s2 = jnp.where(pos <= 1, b2_ref[...], r2)
    conv = s2 * cw_ref[0:1, :] + s1 * cw_ref[1:2, :] + cx * cw_ref[2:3, :]
    y_ref[:, d_a:] = (gb_ref[...] * conv).astype(BF16)

    xo_ref[...] = x_ref[...] + _dot(y_ref[...], wout_ref[...])


def _even_mix(proj, x, vn_g, vn_b, ws_eff, bias, conv_w, w_out, b1, b2, prompt, seq_len, tm, d_a):
    T, D = x.shape
    d_b = D - d_a
    tps = max(seq_len // tm, 1)
    col = lambda k: pl.BlockSpec((tm, d_a), lambda i, k=k: (i, k))
    const = lambda shape: pl.BlockSpec(shape, lambda i: (0,) * len(shape))
    if prompt:
        b1_spec = pl.BlockSpec((1, 2, d_b), lambda i: (i // tps, 0, 0))
        b2_spec = pl.BlockSpec((1, 2, d_b), lambda i: (i // tps, 0, 0))
    else:
        b1_spec = pl.BlockSpec((tm, d_b), lambda i: (i, 0))
        b2_spec = pl.BlockSpec((tm, d_b), lambda i: (i, 0))
    return pl.pallas_call(
        functools.partial(_even_mix_kernel, prompt, seq_len, tps, tm, d_a),
        grid=(T // tm,),
        in_specs=[col(0), col(1), col(2), col(3), col(4),
                  pl.BlockSpec((tm, D), lambda i: (i, 0)),
                  const((1, d_a)), const((1, d_a)),
                  const(ws_eff.shape), const(bias.shape), const(conv_w.shape), const(w_out.shape),
                  b1_spec, b2_spec],
        out_specs=[pl.BlockSpec((tm, D), lambda i: (i, 0)),
                   pl.BlockSpec((tm, d_a), lambda i: (i, 0)),
                   pl.BlockSpec((tm, d_b), lambda i: (i, 0))],
        out_shape=[jax.ShapeDtypeStruct((T, D), F32),
                   jax.ShapeDtypeStruct((T, d_a), F32),
                   jax.ShapeDtypeStruct((T, d_b), F32)],
        scratch_shapes=[pltpu.VMEM((tm, D), BF16), pltpu.VMEM((SUBLANES, d_b), F32)],
        compiler_params=_cparams("arbitrary"),
        name="even_mix",
    )(proj, proj, proj, proj, proj, x, vn_g, vn_b, ws_eff, bias, conv_w, w_out, b1, b2)


def _rwkv_prep_kernel(prompt, seq_len, tiles_per_seq, tm,
                      x_ref, g_ref, mu_ref, b_ref,
                      rkv_ref, xw_ref, xa_ref, xg_ref, h_ref, carry_ref):
    i = pl.program_id(0)
    h = _rms(x_ref[...], g_ref[...])
    h_ref[...] = h
    row = lax.broadcasted_iota(jnp.int32, h.shape, 0)
    r1 = pltpu.roll(h, 1, 0)
    if prompt:
        @pl.when(i % tiles_per_seq == 0)
        def _():
            carry_ref[...] = jnp.zeros_like(carry_ref)
            carry_ref[SUBLANES - 1:SUBLANES, :] = b_ref[0]
        prev = jnp.where(row == 0, carry_ref[SUBLANES - 1:SUBLANES, :], r1)
        carry_ref[...] = h[tm - SUBLANES:tm, :]
    else:
        prev = jnp.where(row % seq_len == 0, b_ref[...], r1)
    xx = prev - h
    rkv_ref[0] = (h + xx * mu_ref[0:1, :]).astype(BF16)
    xw_ref[...] = (h + xx * mu_ref[1:2, :]).astype(BF16)
    rkv_ref[1] = (h + xx * mu_ref[2:3, :]).astype(BF16)
    rkv_ref[2] = (h + xx * mu_ref[3:4, :]).astype(BF16)
    xa_ref[...] = (h + xx * mu_ref[4:5, :]).astype(BF16)
    xg_ref[...] = (h + xx * mu_ref[5:6, :]).astype(BF16)


def _rwkv_prep(x, g, mu, bnd, prompt, seq_len, tm):
    T, D = x.shape
    tps = max(seq_len // tm, 1)
    if prompt:
        b_spec = pl.BlockSpec((1, 1, D), lambda i: (i // tps, 0, 0))
    else:
        b_spec = pl.BlockSpec((tm, D), lambda i: (i, 0))
    tok = pl.BlockSpec((tm, D), lambda i: (i, 0))
    return pl.pallas_call(
        functools.partial(_rwkv_prep_kernel, prompt, seq_len, tps, tm),
        grid=(T // tm,),
        in_specs=[tok, pl.BlockSpec((1, D), lambda i: (0, 0)),
                  pl.BlockSpec(mu.shape, lambda i: (0, 0)), b_spec],
        out_specs=[pl.BlockSpec((3, tm, D), lambda i: (0, i, 0)), tok, tok, tok, tok],
        out_shape=[jax.ShapeDtypeStruct((3, T, D), BF16),
                   jax.ShapeDtypeStruct((T, D), BF16),
                   jax.ShapeDtypeStruct((T, D), BF16),
                   jax.ShapeDtypeStruct((T, D), BF16),
                   jax.ShapeDtypeStruct((T, D), F32)],
        scratch_shapes=[pltpu.VMEM((SUBLANES, D), F32)],
        compiler_params=_cparams("arbitrary"),
        name="rwkv_prep",
    )(x, g, mu, bnd)


def _bmm_kernel(x_ref, w_ref, o_ref):
    o_ref[0] = _dot(x_ref[0], w_ref[0])


def _bmm(x, w, tm):
    nb, T, D = x.shape
    N = w.shape[2]
    return pl.pallas_call(
        _bmm_kernel,
        grid=(nb, T // tm),
        in_specs=[pl.BlockSpec((1, tm, D), lambda b, i: (b, i, 0)),
                  pl.BlockSpec((1, D, N), lambda b, i: (b, 0, 0))],
        out_specs=pl.BlockSpec((1, tm, N), lambda b, i: (b, i, 0)),
        out_shape=jax.ShapeDtypeStruct((nb, T, N), F32),
        compiler_params=_cparams("arbitrary", "arbitrary"),
        name="rwkv_rkv_proj",
    )(x, w)


def _lora_kernel(mid, out, x_ref, w1_ref, w2_ref, b_ref, o_ref):
    t = _dot(x_ref[...], w1_ref[...])
    if mid == "tanh":
        t = jnp.tanh(t)
    elif mid == "sigmoid":
        t = _sigmoid(t)
    z = _dot(t.astype(BF16), w2_ref[...]) + b_ref[...]
    if out == "decay":
        w_log = jnp.minimum(z, 0.0) - jnp.log(1.0 + jnp.exp(-jnp.abs(z))) - 0.5
        z = jnp.exp(-jnp.exp(w_log))
    elif out == "sigmoid":
        z = _sigmoid(z)
    o_ref[...] = z


def _lora(x, w1, w2, b, mid, out, tm):
    T, D = x.shape
    R = w1.shape[1]
    return pl.pallas_call(
        functools.partial(_lora_kernel, mid, out),
        grid=(T // tm,),
        in_specs=[pl.BlockSpec((tm, D), lambda i: (i, 0)),
                  pl.BlockSpec((D, R), lambda i: (0, 0)),
                  pl.BlockSpec((R, D), lambda i: (0, 0)),
                  pl.BlockSpec((1, D), lambda i: (0, 0))],
        out_specs=pl.BlockSpec((tm, D), lambda i: (i, 0)),
        out_shape=jax.ShapeDtypeStruct((T, D), F32),
        compiler_params=_cparams("arbitrary"),
        name="rwkv_lora_" + out,
    )(x, w1, w2, b)


def _wkv_scan_kernel(tb, r_ref, w_ref, k_ref, v_ref, a_ref, s0_ref,
                     kk_c, ka_c, rk_c, gg_c, gb_c, z_ref, st_ref, s_ref, vec_ref):
    n = RW_HEAD_DIM
    step0 = pl.program_id(1)

    @pl.when(step0 == 0)
    def _():
        s_ref[...] = s0_ref[0]

    def step(t, carry):
        r_t = r_ref[0, t]
        w_t = w_ref[0, t]
        k_t = k_ref[0, t]
        v_t = v_ref[0, t]
        a_t = a_ref[0, t]
        kk = k_t * kk_c[...]
        nrm = jnp.sqrt(jnp.sum(kk * kk, axis=0, keepdims=True))
        kk = kk / jnp.maximum(nrm, 1e-12)
        kmod = k_t * (1.0 + (a_t - 1.0) * ka_c[...])
        vec_ref[0] = kk
        vec_ref[1] = kk * a_t
        vec_ref[2] = w_t
        vec_ref[3] = kmod
        vec_ref[4] = r_t

        sa = jnp.zeros((n, LANES), F32)
        for j in range(n):
            sa = sa + s_ref[j] * vec_ref[0, j:j + 1, :]
        y = jnp.zeros((n, LANES), F32)
        for j in range(n):
            s_new = (s_ref[j] * vec_ref[2, j:j + 1, :] - sa * vec_ref[1, j:j + 1, :]
                     + v_t * vec_ref[3, j:j + 1, :])
            s_ref[j] = s_new
            y = y + s_new * vec_ref[4, j:j + 1, :]

        mu = jnp.mean(y, axis=0, keepdims=True)
        yc = y - mu
        var = jnp.mean(yc * yc, axis=0, keepdims=True)
        yn = yc * lax.rsqrt(var + RW_GN_EPS) * gg_c[...] + gb_c[...]
        bonus = jnp.sum(r_t * kmod * rk_c[...], axis=0, keepdims=True) * v_t
        z_ref[0, t] = yn + bonus
        return carry

    lax.fori_loop(0, tb, step, 0)

    @pl.when(step0 == pl.num_programs(1) - 1)
    def _():
        st_ref[0] = s_ref[...]


def _wkv_scan(r, w, k, v, a, s0, consts, tb):
    G, L, n, _ = r.shape
    seq = pl.BlockSpec((1, tb, n, LANES), lambda g, t: (g, t, 0, 0))
    st = pl.BlockSpec((1, n, n, LANES), lambda g, t: (g, 0, 0, 0))
    cst = pl.BlockSpec((n, LANES), lambda g, t: (0, 0))
    return pl.pallas_call(
        functools.partial(_wkv_scan_kernel, tb),
        grid=(G, L // tb),
        in_specs=[seq, seq, seq, seq, seq, st, cst, cst, cst, cst, cst],
        out_specs=[seq, st],
        out_shape=[jax.ShapeDtypeStruct((G, L, n, LANES), F32),
                   jax.ShapeDtypeStruct((G, n, n, LANES), F32)],
        scratch_shapes=[pltpu.VMEM((n, n, LANES), F32), pltpu.VMEM((5, n, LANES), F32)],
        compiler_params=_cparams("arbitrary", "arbitrary"),
        name="rwkv_scan",
    )(r, w, k, v, a, s0, *consts)


def _rwkv_out_kernel(z_ref, g_ref, x_ref, w_ref, o_ref):
    o_ref[...] = x_ref[...] + _dot((z_ref[...] * g_ref[...]).astype(BF16), w_ref[...])


def _rwkv_out(z, g, x, w, tm):
    T, D = x.shape
    tok = pl.BlockSpec((tm, D), lambda i: (i, 0))
    return pl.pallas_call(
        _rwkv_out_kernel,
        grid=(T // tm,),
        in_specs=[tok, tok, tok, pl.BlockSpec((D, D), lambda i: (0, 0))],
        out_specs=tok,
        out_shape=jax.ShapeDtypeStruct((T, D), F32),
        compiler_params=_cparams("arbitrary"),
        name="rwkv_out",
    )(z, g, x, w)


def _moe_route_kernel(x_ref, g_ref, whi_ref, wlo_ref, b_ref, h_ref, gate_ref):
    h = _rms(x_ref[...], g_ref[...])
    h_hi = h.astype(BF16)
    h_lo = (h - h_hi.astype(F32)).astype(BF16)
    h_ref[...] = h_hi
    nt = (((1,), (1,)), ((), ()))
    logits = (lax.dot_general(whi_ref[...], h_hi, nt, preferred_element_type=F32)
              + lax.dot_general(whi_ref[...], h_lo, nt, preferred_element_type=F32)
              + lax.dot_general(wlo_ref[...], h_hi, nt, preferred_element_type=F32)
              + b_ref[...])
    gl = [logits[k:k + 1, :] for k in range(N_GROUPS)]
    gmax = functools.reduce(jnp.maximum, gl)
    sel, taken = [], jnp.zeros_like(gmax)
    for k in range(N_GROUPS):
        s = jnp.where((gl[k] == gmax) & (taken == 0.0), 1.0, 0.0)
        taken = taken + s
        sel.append(s)
    p_group = 1.0 / functools.reduce(jnp.add, [jnp.exp(x - gmax) for x in gl])

    el = []
    for m in range(EXP_PER_GROUP):
        rows = [logits[SUBLANES + k * EXP_PER_GROUP + m:SUBLANES + k * EXP_PER_GROUP + m + 1, :]
                for k in range(N_GROUPS)]
        el.append(functools.reduce(jnp.add, [jnp.where(sel[k] > 0.0, rows[k], 0.0) for k in range(N_GROUPS)]))
    emax = functools.reduce(jnp.maximum, el)
    ee = [jnp.exp(x - emax) for x in el]
    esum = functools.reduce(jnp.add, ee)
    prob = [x / esum for x in ee]

    def first_argmax(vals):
        vmax = functools.reduce(jnp.maximum, vals)
        hot, used = [], jnp.zeros_like(vmax)
        for x in vals:
            s = jnp.where((x == vmax) & (used == 0.0), 1.0, 0.0)
            used = used + s
            hot.append(s)
        return hot, vmax

    t1, p1 = first_argmax(prob)
    rest = [jnp.where(t1[m] > 0.0, -1.0, prob[m]) for m in range(EXP_PER_GROUP)]
    t2, p2 = first_argmax(rest)
    scale = p_group / (p1 + p2)
    for k in range(N_GROUPS):
        for m in range(EXP_PER_GROUP):
            e = k * EXP_PER_GROUP + m
            gate_ref[e:e + 1, :] = sel[k] * (t1[m] * p1 + t2[m] * p2) * scale


def _moe_route(x, g, w_hi, w_lo, b, tm):
    T, D = x.shape
    R = w_hi.shape[0]
    return pl.pallas_call(
        _moe_route_kernel,
        grid=(T // tm,),
        in_specs=[pl.BlockSpec((tm, D), lambda i: (i, 0)),
                  pl.BlockSpec((1, D), lambda i: (0, 0)),
                  pl.BlockSpec((R, D), lambda i: (0, 0)),
                  pl.BlockSpec((R, D), lambda i: (0, 0)),
                  pl.BlockSpec((R, 1), lambda i: (0, 0))],
        out_specs=[pl.BlockSpec((tm, D), lambda i: (i, 0)),
                   pl.BlockSpec((N_EXPERTS, tm), lambda i: (0, i))],
        out_shape=[jax.ShapeDtypeStruct((T, D), BF16),
                   jax.ShapeDtypeStruct((N_EXPERTS, T), F32)],
        compiler_params=_cparams("arbitrary"),
        name="moe_route",
    )(x, g, w_hi, w_lo, b)


def _moe_expert_kernel(h_ref, gates_ref, x_ref, wg_ref, wu_ref, wd_ref, o_ref):
    e = pl.program_id(1)

    @pl.when(e == 0)
    def _():
        o_ref[...] = x_ref[...]

    lane = lax.broadcasted_iota(jnp.int32, gates_ref.shape, 1)
    gate = jnp.sum(jnp.where(lane == e, gates_ref[...], 0.0), axis=1, keepdims=True)
    hg = _dot(h_ref[...], wg_ref[0])
    hu = _dot(h_ref[...], wu_ref[0])
    act = hg * _sigmoid(hg) * hu * gate
    o_ref[...] += _dot(act.astype(BF16), wd_ref[0])


def _moe_experts(h, gates, x, w_gate, w_up, w_down, tm):
    T, D = x.shape
    E, _, Fd = w_gate.shape
    return pl.pallas_call(
        _moe_expert_kernel,
        grid=(T // tm, E),
        in_specs=[pl.BlockSpec((tm, D), lambda i, e: (i, 0)),
                  pl.BlockSpec((tm, E), lambda i, e: (i, 0)),
                  pl.BlockSpec((tm, D), lambda i, e: (i, 0)),
                  pl.BlockSpec((1, D, Fd), lambda i, e: (e, 0, 0)),
                  pl.BlockSpec((1, D, Fd), lambda i, e: (e, 0, 0)),
                  pl.BlockSpec((1, Fd, D), lambda i, e: (e, 0, 0))],
        out_specs=pl.BlockSpec((tm, D), lambda i, e: (i, 0)),
        out_shape=jax.ShapeDtypeStruct((T, D), F32),
        compiler_params=_cparams("arbitrary", "arbitrary"),
        name="moe_experts",
    )(h, gates, x, w_gate, w_up, w_down)


def _ple_kernel(x_ref, xr_ref, p_ref, g_ref, wg_ref, wp_ref, o_ref, h_ref):
    @pl.when(pl.program_id(1) == 0)
    def _():
        h_ref[...] = _rms(x_ref[...], g_ref[...]).astype(BF16)

    gate = _sigmoid(_dot(h_ref[...], wg_ref[...]))
    o_ref[...] = xr_ref[...] + gate * _dot(p_ref[...].astype(BF16), wp_ref[...])


def _ple(x, p, g, w_gate, w_proj, tm, tn):
    T, D = x.shape
    P = p.shape[1]
    return pl.pallas_call(
        _ple_kernel,
        grid=(T // tm, D // tn),
        in_specs=[pl.BlockSpec((tm, D), lambda i, n: (i, 0)),
                  pl.BlockSpec((tm, tn), lambda i, n: (i, n)),
                  pl.BlockSpec((tm, P), lambda i, n: (i, 0)),
                  pl.BlockSpec((1, D), lambda i, n: (0, 0)),
                  pl.BlockSpec((D, tn), lambda i, n: (0, n)),
                  pl.BlockSpec((P, tn), lambda i, n: (0, n))],
        out_specs=pl.BlockSpec((tm, tn), lambda i, n: (i, n)),
        out_shape=jax.ShapeDtypeStruct((T, D), F32),
        scratch_shapes=[pltpu.VMEM((tm, D), BF16)],
        compiler_params=_cparams("arbitrary", "arbitrary"),
        name="ple",
    )(x, x, p, g, w_gate, w_proj)


def _final_norm_kernel(x_ref, g_ref, o_ref):
    o_ref[...] = _rms(x_ref[...], g_ref[...])


def _final_norm(x, g, tm):
    T, D = x.shape
    return pl.pallas_call(
        _final_norm_kernel,
        grid=(T // tm,),
        in_specs=[pl.BlockSpec((tm, D), lambda i: (i, 0)), pl.BlockSpec((1, D), lambda i: (0, 0))],
        out_specs=pl.BlockSpec((tm, D), lambda i: (i, 0)),
        out_shape=jax.ShapeDtypeStruct((T, D), F32),
        compiler_params=_cparams("arbitrary"),
        name="final_norm",
    )(x, g)


def _to_scan(a, B, L, H):
    bg = LANES // H
    a = a.reshape(a.shape[:-2] + (B // bg, bg, L, H, RW_HEAD_DIM))
    nd = a.ndim
    perm = tuple(range(nd - 5)) + (nd - 5, nd - 3, nd - 1, nd - 4, nd - 2)
    return a.transpose(perm).reshape(a.shape[:nd - 5] + (B // bg, L, RW_HEAD_DIM, LANES))


def _from_scan(z, B, L, H):
    bg = LANES // H
    z = z.reshape(B // bg, L, RW_HEAD_DIM, bg, H)
    return z.transpose(0, 3, 1, 4, 2).reshape(B * L, H * RW_HEAD_DIM)


def _chain_tile(p, H):
    return jnp.tile(p.reshape(H, RW_HEAD_DIM).T, (1, LANES // H))


def _trunk(x, p, conv0, shift0, wkv0, W, prompt):
    B, L, D = x.shape
    depth = p.shape[0]
    T = B * L
    d_a = W["vn_g"].shape[1]
    H = D // RW_HEAD_DIM
    bg = LANES // H
    lc = min(L, CHUNK)
    x = x.reshape(T, D)
    tm = 512
    tm_mix = 256
    chunk_v, conv_new, shift_new, wkv_new = [], [], [], []
    for i in range(depth):
        j = i // 2
        if i % 2 == 0:
            proj = _even_inproj(x, W["g_mix"][i][None], W["w_in"][j], d_a, tm, d_a)
            ws_eff = jnp.tile(W["w_s"][j][:, :lc, :lc], (1, CHUNK // lc, CHUNK // lc))
            bias = jnp.tile(jnp.repeat(W["b_s"][j][:, :lc].T, A_GROUP_DIM, axis=1), (CHUNK // lc, 1))
            st = conv0[j]
            if prompt:
                b1 = b2 = st
            else:
                zero = jnp.zeros((B, L, st.shape[-1]), F32)
                b1 = zero.at[:, 0].set(st[:, 1]).reshape(T, -1)
                b2 = zero.at[:, 0].set(st[:, 0]).at[:, 1].set(st[:, 1]).reshape(T, -1)
            x, v_all, cx_all = _even_mix(proj, x, W["vn_g"][j][None], W["vn_b"][j][None], ws_eff, bias,
                                         W["conv_w"][j], W["w_out"][j], b1, b2, prompt, L, tm_mix, d_a)
            start = ((L - 1) // CHUNK) * CHUNK
            chunk_v.append(v_all.reshape(B, L, -1)[:, start:])
            conv_new.append(cx_all.reshape(B, L, -1)[:, L - 2:])
        else:
            st = shift0[j]
            if prompt:
                bnd = st[:, None, :]
            else:
                bnd = jnp.zeros((B, L, D), F32).at[:, 0].set(st).reshape(T, D)
            xrkv, xw, xa, xg, h_all = _rwkv_prep(x, W["g_mix"][i][None], W["rw_mu"][j], bnd, prompt, L, tm_mix)
            rkv = _bmm(xrkv, W["rw_w_rkv"][j], tm)
            decay = _lora(xw, W["rw_w1"][j], W["rw_w2"][j], W["rw_w0"][j][None], "tanh", "decay", tm)
            a = _lora(xa, W["rw_a1"][j], W["rw_a2"][j], W["rw_a0"][j][None], "none", "sigmoid", tm)
            g = _lora(xg, W["rw_g1"][j], W["rw_g2"][j], jnp.zeros((1, D), F32), "sigmoid", "none", tm)
            rkv_s = _to_scan(rkv, B, L, H)
            s0 = wkv0[j].reshape(B // bg, bg, H, RW_HEAD_DIM, RW_HEAD_DIM)
            s0 = s0.transpose(0, 4, 3, 1, 2).reshape(B // bg, RW_HEAD_DIM, RW_HEAD_DIM, LANES)
            consts = (_chain_tile(W["rw_k_k"][j], H), _chain_tile(W["rw_k_a"][j], H),
                      _chain_tile(W["rw_r_k"][j].reshape(-1), H),
                      _chain_tile(W["rw_gn_g"][j], H), _chain_tile(W["rw_gn_b"][j], H))
            z, s_t = _wkv_scan(rkv_s[0], _to_scan(decay, B, L, H), rkv_s[1], rkv_s[2], _to_scan(a, B, L, H),
                               s0, consts, min(L, 32))
            x = _rwkv_out(_from_scan(z, B, L, H), g, x, W["rw_w_o"][j], tm_mix)
            shift_new.append(h_all.reshape(B, L, D)[:, -1])
            s_t = s_t.reshape(B // bg, RW_HEAD_DIM, RW_HEAD_DIM, bg, H).transpose(0, 3, 4, 2, 1)
            wkv_new.append(s_t.reshape(B, H, RW_HEAD_DIM, RW_HEAD_DIM))
        h, gates_t = _moe_route(x, W["g_ffn"][i][None], W["moe_r_hi"][i], W["moe_r_lo"][i], W["moe_r_b"][i], tm)
        x = _moe_experts(h, gates_t.T, x, W["moe_w_gate"][i], W["moe_w_up"][i], W["moe_w_down"][i], tm)
        x = _ple(x, p[i].reshape(T, -1), W["ple_g"][i][None], W["ple_w_gate"][i], W["ple_w_proj"][i], tm, 1024)
    y = _final_norm(x, W["g_final"][None], tm).reshape(B, L, D)
    return y, jnp.stack(chunk_v), jnp.stack(conv_new), jnp.stack(shift_new), jnp.stack(wkv_new)


def _pad_rank(w1, w2):
    r = w1.shape[-1]
    rp = -(-r // LANES) * LANES
    w1 = jnp.pad(w1, ((0, 0), (0, 0), (0, rp - r)))
    w2 = jnp.pad(w2, ((0, 0), (0, rp - r), (0, 0)))
    return w1.astype(BF16), w2.astype(BF16)


def kernel(x_prompt, x_sample, state_conv, state_shift, state_wkv, p_prompt, p_sample, g_mix, g_ffn, g_final, w_in_even, vn_g, vn_b, w_s, b_s, conv_w, w_out_even, rw_mu, rw_w_rkv, rw_w0, rw_w1, rw_w2, rw_a0, rw_a1, rw_a2, rw_g1, rw_g2, rw_k_k, rw_k_a, rw_r_k, rw_gn_g, rw_gn_b, rw_w_o, moe_w_gr, moe_b_gr, moe_w_er, moe_b_er, moe_w_gate, moe_w_up, moe_w_down, ple_g, ple_w_gate, ple_w_proj):
    depth, D = g_mix.shape
    r_w = jnp.zeros((depth, 32, D), F32)
    r_w = r_w.at[:, :N_GROUPS].set(jnp.swapaxes(moe_w_gr, 1, 2))
    r_w = r_w.at[:, SUBLANES:SUBLANES + N_EXPERTS].set(jnp.swapaxes(moe_w_er, 1, 2))
    r_hi = r_w.astype(BF16)
    r_lo = (r_w - r_hi.astype(F32)).astype(BF16)
    r_b = jnp.zeros((depth, 32, 1), F32)
    r_b = r_b.at[:, :N_GROUPS, 0].set(moe_b_gr).at[:, SUBLANES:SUBLANES + N_EXPERTS, 0].set(moe_b_er)
    w1, w2 = _pad_rank(rw_w1, rw_w2)
    a1, a2 = _pad_rank(rw_a1, rw_a2)
    g1, g2 = _pad_rank(rw_g1, rw_g2)
    W = dict(g_mix=g_mix, g_ffn=g_ffn, g_final=g_final,
             w_in=w_in_even.astype(BF16), vn_g=vn_g, vn_b=vn_b, w_s=w_s, b_s=b_s, conv_w=conv_w,
             w_out=w_out_even.astype(BF16),
             rw_mu=rw_mu, rw_w_rkv=rw_w_rkv.astype(BF16), rw_w0=rw_w0, rw_w1=w1, rw_w2=w2,
             rw_a0=rw_a0, rw_a1=a1, rw_a2=a2, rw_g1=g1, rw_g2=g2,
             rw_k_k=rw_k_k, rw_k_a=rw_k_a, rw_r_k=rw_r_k, rw_gn_g=rw_gn_g, rw_gn_b=rw_gn_b,
             rw_w_o=rw_w_o.astype(BF16),
             moe_r_hi=r_hi, moe_r_lo=r_lo, moe_r_b=r_b,
             moe_w_gate=moe_w_gate.astype(BF16), moe_w_up=moe_w_up.astype(BF16),
             moe_w_down=moe_w_down.astype(BF16),
             ple_g=ple_g, ple_w_gate=ple_w_gate.astype(BF16), ple_w_proj=ple_w_proj.astype(BF16))
    bp = x_prompt.shape[0]
    n_even, n_odd = state_conv.shape[0], state_shift.shape[0]
    conv0 = jnp.zeros((n_even, bp) + state_conv.shape[2:], F32)
    shift0 = jnp.zeros((n_odd, bp, D), F32)
    wkv0 = jnp.zeros((n_odd, bp) + state_wkv.shape[2:], F32)
    y_p, cv_p, conv_p, shift_p, wkv_p = _trunk(x_prompt, p_prompt, conv0, shift0, wkv0, W, True)
    y_s, cv_s, conv_s, shift_s, wkv_s = _trunk(x_sample, p_sample, state_conv, state_shift, state_wkv, W, False)
    return (y_p, y_s, cv_p, conv_p, shift_p, wkv_p, cv_s, conv_s, shift_s, wkv_s)
```

```python
import functools

import jax
import jax.numpy as jnp
from jax import lax
from jax.experimental import pallas as pl
from jax.experimental.pallas import tpu as pltpu

F32 = jnp.float32
BF16 = jnp.bfloat16

NORM_EPS = 1e-6
LN_EPS = 1e-5
RW_GN_EPS = 64e-5
CHUNK = 128
A_GROUP_DIM = 128
RW_HEAD_DIM = 64
N_GROUPS = 4
EXP_PER_GROUP = 4
N_EXPERTS = N_GROUPS * EXP_PER_GROUP

LANES = 128
SUBLANES = 8
VMEM_LIMIT = 52 * 1024 * 1024


def _cparams(*sem):
    return pltpu.CompilerParams(dimension_semantics=sem, vmem_limit_bytes=VMEM_LIMIT)


def _rms(x, g):
    return x * lax.rsqrt(jnp.mean(x * x, axis=-1, keepdims=True) + NORM_EPS) * g


def _gelu_tanh(x):
    return 0.5 * x * (1.0 + jnp.tanh(0.7978845608028654 * (x + 0.044715 * (x * x * x))))


def _sigmoid(x):
    return 1.0 / (1.0 + jnp.exp(-x))


def _dot(a, b):
    return jnp.dot(a, b, preferred_element_type=F32)


def _even_inproj_kernel(n_gelu, x_ref, g_ref, w_ref, o_ref, h_ref):
    n = pl.program_id(1)

    @pl.when(n == 0)
    def _():
        h_ref[...] = _rms(x_ref[...], g_ref[...]).astype(BF16)

    acc = _dot(h_ref[...], w_ref[...])

    @pl.when(n < n_gelu)
    def _():
        o_ref[...] = _gelu_tanh(acc)

    @pl.when(n >= n_gelu)
    def _():
        o_ref[...] = acc


def _even_inproj(x, g, w, d_a, tm, tn):
    T, D = x.shape
    N = w.shape[1]
    return pl.pallas_call(
        functools.partial(_even_inproj_kernel, (2 * d_a) // tn),
        grid=(T // tm, N // tn),
        in_specs=[pl.BlockSpec((tm, D), lambda i, n: (i, 0)),
                  pl.BlockSpec((1, D), lambda i, n: (0, 0)),
                  pl.BlockSpec((D, tn), lambda i, n: (0, n))],
        out_specs=pl.BlockSpec((tm, tn), lambda i, n: (i, n)),
        out_shape=jax.ShapeDtypeStruct((T, N), F32),
        scratch_shapes=[pltpu.VMEM((tm, D), BF16)],
        compiler_params=_cparams("arbitrary", "arbitrary"),
        name="even_inproj",
    )(x, g, w)


def _even_mix_kernel(prompt, seq_len, tiles_per_seq, tm, d_a,
                     u_ref, v_ref, xi_ref, gb_ref, gc_ref, x_ref, vng_ref, vnb_ref,
                     ws_ref, bias_ref, cw_ref, wout_ref, b1_ref, b2_ref,
                     xo_ref, vo_ref, cxo_ref, y_ref, carry_ref):
    i = pl.program_id(0)
    lc = min(seq_len, CHUNK)

    vg = v_ref[...]
    mu = jnp.mean(vg, axis=-1, keepdims=True)
    vc = vg - mu
    var = jnp.mean(vc * vc, axis=-1, keepdims=True)
    vln = vc * lax.rsqrt(var + LN_EPS) * vng_ref[...] + vnb_ref[...]
    vo_ref[...] = vln

    r = lax.broadcasted_iota(jnp.int32, (CHUNK, CHUNK), 0)
    c = lax.broadcasted_iota(jnp.int32, (CHUNK, CHUNK), 1)
    keep = (r >= c) & ((r // lc) == (c // lc))
    n_groups = d_a // A_GROUP_DIM
    for g in range(n_groups):
        wg = jnp.where(keep, ws_ref[g], 0.0).astype(BF16)
        cols = slice(g * A_GROUP_DIM, (g + 1) * A_GROUP_DIM)
        for ch in range(tm // CHUNK):
            rows = slice(ch * CHUNK, (ch + 1) * CHUNK)
            mixed = _dot(wg, vln[rows, cols].astype(BF16)) + bias_ref[:, cols]
            y_ref[rows, cols] = (u_ref[rows, cols] * mixed).astype(BF16)

    cx = gc_ref[...] * xi_ref[...]
    cxo_ref[...] = cx
    row = lax.broadcasted_iota(jnp.int32, cx.shape, 0)
    r1 = pltpu.roll(cx, 1, 0)
    r2 = pltpu.roll(cx, 2, 0)
    if prompt:
        @pl.when(i % tiles_per_seq == 0)
        def _():
            carry_ref[...] = jnp.zeros_like(carry_ref)
            carry_ref[SUBLANES - 2:SUBLANES, :] = b1_ref[0]
        p1 = carry_ref[SUBLANES - 1:SUBLANES, :]
        p2 = carry_ref[SUBLANES - 2:SUBLANES - 1, :]
        s1 = jnp.where(row == 0, p1, r1)
        s2 = jnp.where(row == 0, p2, jnp.where(row == 1, p1, r2))
        carry_ref[...] = cx[tm - SUBLANES:tm, :]
    else:
        pos = row % seq_len
        s1 = jnp.where(pos == 0, b1_ref[...], r1)
        s2 = jnp.where(pos <= 1, b2_ref[...], r2)
    conv = s2 * cw_ref[0:1, :] + s1 * cw_ref[1:2, :] + cx * cw_ref[2:3, :]
    y_ref[:, d_a:] = (gb_ref[...] * conv).astype(BF16)

    xo_ref[...] = x_ref[...] + _dot(y_ref[...], wout_ref[...])


def _even_mix(proj, x, vn_g, vn_b, ws_eff, bias, conv_w, w_out, b1, b2, prompt, seq_len, tm, d_a):
    T, D = x.shape
    d_b = D - d_a
    tps = max(seq_len // tm, 1)
    col = lambda k: pl.BlockSpec((tm, d_a), lambda i, k=k: (i, k))
    const = lambda shape: pl.BlockSpec(shape, lambda i: (0,) * len(shape))
    if prompt:
        b1_spec = pl.BlockSpec((1, 2, d_b), lambda i: (i // tps, 0, 0))
        b2_spec = pl.BlockSpec((1, 2, d_b), lambda i: (i // tps, 0, 0))
    else:
        b1_spec = pl.BlockSpec((tm, d_b), lambda i: (i, 0))
        b2_spec = pl.BlockSpec((tm, d_b), lambda i: (i, 0))
    return pl.pallas_call(
        functools.partial(_even_mix_kernel, prompt, seq_len, tps, tm, d_a),
        grid=(T // tm,),
        in_specs=[col(0), col(1), col(2), col(3), col(4),
                  pl.BlockSpec((tm, D), lambda i: (i, 0)),
                  const((1, d_a)), const((1, d_a)),
                  const(ws_eff.shape), const(bias.shape), const(conv_w.shape), const(w_out.shape),
                  b1_spec, b2_spec],
        out_specs=[pl.BlockSpec((tm, D), lambda i: (i, 0)),
                   pl.BlockSpec((tm, d_a), lambda i: (i, 0)),
                   pl.BlockSpec((tm, d_b), lambda i: (i, 0))],
        out_shape=[jax.ShapeDtypeStruct((T, D), F32),
                   jax.ShapeDtypeStruct((T, d_a), F32),
                   jax.ShapeDtypeStruct((T, d_b), F32)],
        scratch_shapes=[pltpu.VMEM((tm, D), BF16), pltpu.VMEM((SUBLANES, d_b), F32)],
        compiler_params=_cparams("arbitrary"),
        name="even_mix",
    )(proj, proj, proj, proj, proj, x, vn_g, vn_b, ws_eff, bias, conv_w, w_out, b1, b2)


def _rwkv_prep_kernel(prompt, seq_len, tiles_per_seq, tm,
                      x_ref, g_ref, mu_ref, b_ref,
                      rkv_ref, xw_ref, xa_ref, xg_ref, h_ref, carry_ref):
    i = pl.program_id(0)
    h = _rms(x_ref[...], g_ref[...])
    h_ref[...] = h
    row = lax.broadcasted_iota(jnp.int32, h.shape, 0)
    r1 = pltpu.roll(h, 1, 0)
    if prompt:
        @pl.when(i % tiles_per_seq == 0)
        def _():
            carry_ref[...] = jnp.zeros_like(carry_ref)
            carry_ref[SUBLANES - 1:SUBLANES, :] = b_ref[0]
        prev = jnp.where(row == 0, carry_ref[SUBLANES - 1:SUBLANES, :], r1)
        carry_ref[...] = h[tm - SUBLANES:tm, :]
    else:
        prev = jnp.where(row % seq_len == 0, b_ref[...], r1)
    xx = prev - h
    rkv_ref[0] = (h + xx * mu_ref[0:1, :]).astype(BF16)
    xw_ref[...] = (h + xx * mu_ref[1:2, :]).astype(BF16)
    rkv_ref[1] = (h + xx * mu_ref[2:3, :]).astype(BF16)
    rkv_ref[2] = (h + xx * mu_ref[3:4, :]).astype(BF16)
    xa_ref[...] = (h + xx * mu_ref[4:5, :]).astype(BF16)
    xg_ref[...] = (h + xx * mu_ref[5:6, :]).astype(BF16)


def _rwkv_prep(x, g, mu, bnd, prompt, seq_len, tm):
    T, D = x.shape
    tps = max(seq_len // tm, 1)
    if prompt:
        b_spec = pl.BlockSpec((1, 1, D), lambda i: (i // tps, 0, 0))
    else:
        b_spec = pl.BlockSpec((tm, D), lambda i: (i, 0))
    tok = pl.BlockSpec((tm, D), lambda i: (i, 0))
    return pl.pallas_call(
        functools.partial(_rwkv_prep_kernel, prompt, seq_len, tps, tm),
        grid=(T // tm,),
        in_specs=[tok, pl.BlockSpec((1, D), lambda i: (0, 0)),
                  pl.BlockSpec(mu.shape, lambda i: (0, 0)), b_spec],
        out_specs=[pl.BlockSpec((3, tm, D), lambda i: (0, i, 0)), tok, tok, tok, tok],
        out_shape=[jax.ShapeDtypeStruct((3, T, D), BF16),
                   jax.ShapeDtypeStruct((T, D), BF16),
                   jax.ShapeDtypeStruct((T, D), BF16),
                   jax.ShapeDtypeStruct((T, D), BF16),
                   jax.ShapeDtypeStruct((T, D), F32)],
        scratch_shapes=[pltpu.VMEM((SUBLANES, D), F32)],
        compiler_params=_cparams("arbitrary"),
        name="rwkv_prep",
    )(x, g, mu, bnd)


HEAD_PAIRS = 16
GROUP_BATCH = 4


def _store_scan_rows(o_ref, lead, acc, nb, lt):
    if nb == 1:
        o_ref[lead + (0,)] = acc.reshape(lt, HEAD_PAIRS, LANES)
    else:
        a5 = acc.reshape(nb // GROUP_BATCH, GROUP_BATCH, lt, HEAD_PAIRS, LANES)
        for bl in range(GROUP_BATCH):
            o_ref[lead + (slice(None), slice(None), slice(bl * HEAD_PAIRS, (bl + 1) * HEAD_PAIRS))] = a5[:, bl]


def _load_scan_rows(z_ref, nb, lt):
    if nb == 1:
        return z_ref[0].reshape(lt, HEAD_PAIRS * LANES)
    parts = [z_ref[:, :, bl * HEAD_PAIRS:(bl + 1) * HEAD_PAIRS, :] for bl in range(GROUP_BATCH)]
    return jnp.stack(parts, axis=1).reshape(nb * lt, HEAD_PAIRS * LANES)


def _scan_rows_spec(nb, lt, tps, lead=()):
    nl = len(lead)
    if nb == 1:
        shape = (1,) * nl + (1, lt, HEAD_PAIRS, LANES)
        return shape, lambda i: (i // tps // GROUP_BATCH, i % tps, (i // tps) % GROUP_BATCH, 0)
    shape = (1,) * nl + (nb // GROUP_BATCH, lt, GROUP_BATCH * HEAD_PAIRS, LANES)
    return shape, lambda i: (i, 0, 0, 0)


def _tile_geometry(tm, seq_len):
    lt = min(tm, seq_len)
    return tm // lt, lt, seq_len // lt


def _bmm_kernel(nb, lt, x_ref, w_ref, o_ref):
    _store_scan_rows(o_ref, (0,), _dot(x_ref[0], w_ref[0]), nb, lt)


def _bmm(x, w, tm, B, L):
    n3, T, D = x.shape
    N = w.shape[2]
    nb, lt, tps = _tile_geometry(tm, L)
    shape, idx = _scan_rows_spec(nb, lt, tps, lead=(0,))
    return pl.pallas_call(
        functools.partial(_bmm_kernel, nb, lt),
        grid=(n3, T // tm),
        in_specs=[pl.BlockSpec((1, tm, D), lambda b, i: (b, i, 0)),
                  pl.BlockSpec((1, D, N), lambda b, i: (b, 0, 0))],
        out_specs=pl.BlockSpec(shape, lambda b, i: (b,) + idx(i)),
        out_shape=jax.ShapeDtypeStruct((n3, B // GROUP_BATCH, L, GROUP_BATCH * HEAD_PAIRS, LANES), F32),
        compiler_params=_cparams("arbitrary", "arbitrary"),
        name="rwkv_rkv_proj",
    )(x, w)


def _lora_kernel(mid, out, geom, x_ref, w1_ref, w2_ref, b_ref, o_ref):
    t = _dot(x_ref[...], w1_ref[...])
    if mid == "tanh":
        t = jnp.tanh(t)
    elif mid == "sigmoid":
        t = _sigmoid(t)
    z = _dot(t.astype(BF16), w2_ref[...]) + b_ref[...]
    if out == "decay":
        w_log = jnp.minimum(z, 0.0) - jnp.log(1.0 + jnp.exp(-jnp.abs(z))) - 0.5
        z = jnp.exp(-jnp.exp(w_log))
    elif out == "sigmoid":
        z = _sigmoid(z)
    if geom is None:
        o_ref[...] = z
    else:
        _store_scan_rows(o_ref, (), z, *geom)


def _lora(x, w1, w2, b, mid, out, tm, head_pairs=None):
    T, D = x.shape
    R = w1.shape[1]
    if head_pairs is None:
        geom = None
        out_spec = pl.BlockSpec((tm, D), lambda i: (i, 0))
        out_shape = jax.ShapeDtypeStruct((T, D), F32)
    else:
        B, L = head_pairs
        nb, lt, tps = _tile_geometry(tm, L)
        geom = (nb, lt)
        out_spec = pl.BlockSpec(*_scan_rows_spec(nb, lt, tps))
        out_shape = jax.ShapeDtypeStruct((B // GROUP_BATCH, L, GROUP_BATCH * HEAD_PAIRS, LANES), F32)
    return pl.pallas_call(
        functools.partial(_lora_kernel, mid, out, geom),
        grid=(T // tm,),
        in_specs=[pl.BlockSpec((tm, D), lambda i: (i, 0)),
                  pl.BlockSpec((D, R), lambda i: (0, 0)),
                  pl.BlockSpec((R, D), lambda i: (0, 0)),
                  pl.BlockSpec((1, D), lambda i: (0, 0))],
        out_specs=out_spec,
        out_shape=out_shape,
        compiler_params=_cparams("arbitrary"),
        name="rwkv_lora_" + out,
    )(x, w1, w2, b)


def _wkv_scan_kernel(tb, r_ref, k_ref, v_ref, w_ref, a_ref, s0_ref,
                     kk_c, ka_c, rk_c, gg_c, gb_c, z_ref, st_ref, s_ref, tile_ref, vec_ref, zbuf_ref):
    n = RW_HEAD_DIM
    half = LANES // 2
    step0 = pl.program_id(1)
    seq_refs = (r_ref, w_ref, k_ref, v_ref, a_ref)
    R, W, K, V, A = range(5)

    @pl.when(step0 == 0)
    def _():
        zbuf_ref[...] = jnp.zeros_like(zbuf_ref)
        for c in range(n * n // LANES):
            xt = s0_ref[:, c * LANES:(c + 1) * LANES].T
            for il in range(LANES // n):
                s_ref[pl.ds(c * (LANES // n) + il, n, stride=n), :] = xt[il * n:(il + 1) * n]

    def fetch(t, slot):
        for q, ref in enumerate(seq_refs):
            xt = ref[(0,) * (len(ref.shape) - 3) + (t,)].T
            tile_ref[slot, q] = jnp.concatenate([xt[:n], xt[n:]], axis=1)

    def flush(t, slot):
        z = zbuf_ref[slot]
        z_ref[0, t] = jnp.concatenate([z[:, :half], z[:, half:]], axis=0).T

    def recur(slot):
        k_t = tile_ref[slot, K]
        a_t = tile_ref[slot, A]
        v_t = tile_ref[slot, V]
        kk = k_t * kk_c[...]
        nrm = jnp.sqrt(jnp.sum(kk * kk, axis=0, keepdims=True))
        kk = kk / jnp.maximum(nrm, 1e-12)
        kmod = k_t * (1.0 + (a_t - 1.0) * ka_c[...])
        vec_ref[slot, 0] = kk
        vec_ref[slot, 1] = kk * a_t
        vec_ref[slot, 2] = kmod

        sa = jnp.zeros((n, LANES), F32)
        for j in range(n):
            sa = sa + s_ref[j * n:(j + 1) * n, :] * vec_ref[slot, 0, j:j + 1, :]
        y = jnp.zeros((n, LANES), F32)
        for j in range(n):
            s_new = (s_ref[j * n:(j + 1) * n, :] * tile_ref[slot, W, j:j + 1, :]
                     - sa * vec_ref[slot, 1, j:j + 1, :] + v_t * vec_ref[slot, 2, j:j + 1, :])
            s_ref[j * n:(j + 1) * n, :] = s_new
            y = y + s_new * tile_ref[slot, R, j:j + 1, :]

        mu = jnp.mean(y, axis=0, keepdims=True)
        yc = y - mu
        var = jnp.mean(yc * yc, axis=0, keepdims=True)
        yn = yc * lax.rsqrt(var + RW_GN_EPS) * gg_c[...] + gb_c[...]
        bonus = jnp.sum(tile_ref[slot, R] * kmod * rk_c[...], axis=0, keepdims=True) * v_t
        zbuf_ref[slot] = yn + bonus

    fetch(0, 0)

    def pair(u, carry):
        t = 2 * u
        fetch(t + 1, 1)
        flush(jnp.maximum(t - 1, 0), 1)
        recur(0)
        fetch(jnp.minimum(t + 2, tb - 1), 0)
        flush(t, 0)
        recur(1)
        return carry

    lax.fori_loop(0, tb // 2, pair, 0)
    flush(tb - 1, 1)

    @pl.when(step0 == pl.num_programs(1) - 1)
    def _():
        for c in range(n * n // LANES):
            xt = jnp.concatenate([s_ref[pl.ds(c * (LANES // n) + il, n, stride=n), :]
                                  for il in range(LANES // n)], axis=0)
            st_ref[:, c * LANES:(c + 1) * LANES] = xt.T


def _wkv_scan(rkv, w, a, s0, consts, tb):
    _, G, L, half, _ = rkv.shape
    n = RW_HEAD_DIM
    assert tb % 2 == 0 and L % tb == 0 and half == LANES // 2
    seq3 = lambda m: pl.BlockSpec((1, 1, tb, half, LANES), lambda g, t, m=m: (m, g, t, 0, 0))
    seq = pl.BlockSpec((1, tb, half, LANES), lambda g, t: (g, t, 0, 0))
    st = pl.BlockSpec((LANES, n * n), lambda g, t: (g, 0))
    cst = pl.BlockSpec((n, LANES), lambda g, t: (0, 0))
    return pl.pallas_call(
        functools.partial(_wkv_scan_kernel, tb),
        grid=(G, L // tb),
        in_specs=[seq3(0), seq3(1), seq3(2), seq, seq, st, cst, cst, cst, cst, cst],
        out_specs=[seq, st],
        out_shape=[jax.ShapeDtypeStruct((G, L, half, LANES), F32),
                   jax.ShapeDtypeStruct((G * LANES, n * n), F32)],
        scratch_shapes=[pltpu.VMEM((n * n, LANES), F32), pltpu.VMEM((2, 5, n, LANES), F32),
                        pltpu.VMEM((2, 3, n, LANES), F32), pltpu.VMEM((2, n, LANES), F32)],
        compiler_params=_cparams("arbitrary", "arbitrary"),
        name="rwkv_scan",
    )(rkv, rkv, rkv, w, a, s0, *consts)


def _rwkv_out_kernel(nb, lt, z_ref, g_ref, x_ref, w_ref, o_ref):
    z = _load_scan_rows(z_ref, nb, lt)
    o_ref[...] = x_ref[...] + _dot((z * g_ref[...]).astype(BF16), w_ref[...])


def _rwkv_out(z, g, x, w, tm, L):
    T, D = x.shape
    nb, lt, tps = _tile_geometry(tm, L)
    tok = pl.BlockSpec((tm, D), lambda i: (i, 0))
    return pl.pallas_call(
        functools.partial(_rwkv_out_kernel, nb, lt),
        grid=(T // tm,),
        in_specs=[pl.BlockSpec(*_scan_rows_spec(nb, lt, tps)),
                  tok, tok, pl.BlockSpec((D, D), lambda i: (0, 0))],
        out_specs=tok,
        out_shape=jax.ShapeDtypeStruct((T, D), F32),
        compiler_params=_cparams("arbitrary"),
        name="rwkv_out",
    )(z, g, x, w)


def _moe_route_kernel(x_ref, g_ref, whi_ref, wlo_ref, b_ref, h_ref, gate_ref):
    h = _rms(x_ref[...], g_ref[...])
    h_hi = h.astype(BF16)
    h_lo = (h - h_hi.astype(F32)).astype(BF16)
    h_ref[...] = h_hi
    nt = (((1,), (1,)), ((), ()))
    logits = (lax.dot_general(whi_ref[...], h_hi, nt, preferred_element_type=F32)
              + lax.dot_general(whi_ref[...], h_lo, nt, preferred_element_type=F32)
              + lax.dot_general(wlo_ref[...], h_hi, nt, preferred_element_type=F32)
              + b_ref[...])
    gl = [logits[k:k + 1, :] for k in range(N_GROUPS)]
    gmax = functools.reduce(jnp.maximum, gl)
    sel, taken = [], jnp.zeros_like(gmax)
    for k in range(N_GROUPS):
        s = jnp.where((gl[k] == gmax) & (taken == 0.0), 1.0, 0.0)
        taken = taken + s
        sel.append(s)
    p_group = 1.0 / functools.reduce(jnp.add, [jnp.exp(x - gmax) for x in gl])

    el = []
    for m in range(EXP_PER_GROUP):
        rows = [logits[SUBLANES + k * EXP_PER_GROUP + m:SUBLANES + k * EXP_PER_GROUP + m + 1, :]
                for k in range(N_GROUPS)]
        el.append(functools.reduce(jnp.add, [jnp.where(sel[k] > 0.0, rows[k], 0.0) for k in range(N_GROUPS)]))
    emax = functools.reduce(jnp.maximum, el)
    ee = [jnp.exp(x - emax) for x in el]
    esum = functools.reduce(jnp.add, ee)
    prob = [x / esum for x in ee]

    def first_argmax(vals):
        vmax = functools.reduce(jnp.maximum, vals)
        hot, used = [], jnp.zeros_like(vmax)
        for x in vals:
            s = jnp.where((x == vmax) & (used == 0.0), 1.0, 0.0)
            used = used + s
            hot.append(s)
        return hot, vmax

    t1, p1 = first_argmax(prob)
    rest = [jnp.where(t1[m] > 0.0, -1.0, prob[m]) for m in range(EXP_PER_GROUP)]
    t2, p2 = first_argmax(rest)
    scale = p_group / (p1 + p2)
    for k in range(N_GROUPS):
        for m in range(EXP_PER_GROUP):
            e = k * EXP_PER_GROUP + m
            gate_ref[e:e + 1, :] = sel[k] * (t1[m] * p1 + t2[m] * p2) * scale


def _moe_route(x, g, w_hi, w_lo, b, tm):
    T, D = x.shape
    R = w_hi.shape[0]
    return pl.pallas_call(
        _moe_route_kernel,
        grid=(T // tm,),
        in_specs=[pl.BlockSpec((tm, D), lambda i: (i, 0)),
                  pl.BlockSpec((1, D), lambda i: (0, 0)),
                  pl.BlockSpec((R, D), lambda i: (0, 0)),
                  pl.BlockSpec((R, D), lambda i: (0, 0)),
                  pl.BlockSpec((R, 1), lambda i: (0, 0))],
        out_specs=[pl.BlockSpec((tm, D), lambda i: (i, 0)),
                   pl.BlockSpec((N_EXPERTS, tm), lambda i: (0, i))],
        out_shape=[jax.ShapeDtypeStruct((T, D), BF16),
                   jax.ShapeDtypeStruct((N_EXPERTS, T), F32)],
        compiler_params=_cparams("arbitrary"),
        name="moe_route",
    )(x, g, w_hi, w_lo, b)


def _moe_expert_kernel(h_ref, gates_ref, x_ref, wg_ref, wu_ref, wd_ref, o_ref):
    e = pl.program_id(1)

    @pl.when(e == 0)
    def _():
        o_ref[...] = x_ref[...]

    lane = lax.broadcasted_iota(jnp.int32, gates_ref.shape, 1)
    gate = jnp.sum(jnp.where(lane == e, gates_ref[...], 0.0), axis=1, keepdims=True)
    hg = _dot(h_ref[...], wg_ref[0])
    hu = _dot(h_ref[...], wu_ref[0])
    act = hg * _sigmoid(hg) * hu * gate
    o_ref[...] += _dot(act.astype(BF16), wd_ref[0])


def _moe_experts(h, gates, x, w_gate, w_up, w_down, tm):
    T, D = x.shape
    E, _, Fd = w_gate.shape
    return pl.pallas_call(
        _moe_expert_kernel,
        grid=(T // tm, E),
        in_specs=[pl.BlockSpec((tm, D), lambda i, e: (i, 0)),
                  pl.BlockSpec((tm, E), lambda i, e: (i, 0)),
                  pl.BlockSpec((tm, D), lambda i, e: (i, 0)),
                  pl.BlockSpec((1, D, Fd), lambda i, e: (e, 0, 0)),
                  pl.BlockSpec((1, D, Fd), lambda i, e: (e, 0, 0)),
                  pl.BlockSpec((1, Fd, D), lambda i, e: (e, 0, 0))],
        out_specs=pl.BlockSpec((tm, D), lambda i, e: (i, 0)),
        out_shape=jax.ShapeDtypeStruct((T, D), F32),
        compiler_params=_cparams("arbitrary", "arbitrary"),
        name="moe_experts",
    )(h, gates, x, w_gate, w_up, w_down)


def _ple_kernel(x_ref, xr_ref, p_ref, g_ref, wg_ref, wp_ref, o_ref, h_ref):
    @pl.when(pl.program_id(1) == 0)
    def _():
        h_ref[...] = _rms(x_ref[...], g_ref[...]).astype(BF16)

    gate = _sigmoid(_dot(h_ref[...], wg_ref[...]))
    o_ref[...] = xr_ref[...] + gate * _dot(p_ref[...].astype(BF16), wp_ref[...])


def _ple(x, p, g, w_gate, w_proj, tm, tn):
    T, D = x.shape
    P = p.shape[1]
    return pl.pallas_call(
        _ple_kernel,
        grid=(T // tm, D // tn),
        in_specs=[pl.BlockSpec((tm, D), lambda i, n: (i, 0)),
                  pl.BlockSpec((tm, tn), lambda i, n: (i, n)),
                  pl.BlockSpec((tm, P), lambda i, n: (i, 0)),
                  pl.BlockSpec((1, D), lambda i, n: (0, 0)),
                  pl.BlockSpec((D, tn), lambda i, n: (0, n)),
                  pl.BlockSpec((P, tn), lambda i, n: (0, n))],
        out_specs=pl.BlockSpec((tm, tn), lambda i, n: (i, n)),
        out_shape=jax.ShapeDtypeStruct((T, D), F32),
        scratch_shapes=[pltpu.VMEM((tm, D), BF16)],
        compiler_params=_cparams("arbitrary", "arbitrary"),
        name="ple",
    )(x, x, p, g, w_gate, w_proj)


def _final_norm_kernel(x_ref, g_ref, o_ref):
    o_ref[...] = _rms(x_ref[...], g_ref[...])


def _final_norm(x, g, tm):
    T, D = x.shape
    return pl.pallas_call(
        _final_norm_kernel,
        grid=(T // tm,),
        in_specs=[pl.BlockSpec((tm, D), lambda i: (i, 0)), pl.BlockSpec((1, D), lambda i: (0, 0))],
        out_specs=pl.BlockSpec((tm, D), lambda i: (i, 0)),
        out_shape=jax.ShapeDtypeStruct((T, D), F32),
        compiler_params=_cparams("arbitrary"),
        name="final_norm",
    )(x, g)


def _chain_tile(p, H):
    t = p.reshape(H // 2, 2, RW_HEAD_DIM).transpose(2, 1, 0)
    bg = LANES // H
    return jnp.broadcast_to(t[:, :, None, :], (RW_HEAD_DIM, 2, bg, H // 2)).reshape(RW_HEAD_DIM, LANES)


def _state_to_chains(s, H):
    B = s.shape[0]
    bg = LANES // H
    s = s.reshape(B // bg, bg, H // 2, 2, RW_HEAD_DIM * RW_HEAD_DIM)
    return s.transpose(0, 3, 1, 2, 4).reshape(B * H, RW_HEAD_DIM * RW_HEAD_DIM)


def _state_from_chains(s, B, H):
    bg = LANES // H
    s = s.reshape(B // bg, 2, bg, H // 2, RW_HEAD_DIM, RW_HEAD_DIM)
    return s.transpose(0, 2, 3, 1, 4, 5).reshape(B, H, RW_HEAD_DIM, RW_HEAD_DIM)


def _trunk(x, p, conv0, shift0, wkv0, W, prompt):
    B, L, D = x.shape
    depth = p.shape[0]
    T = B * L
    d_a = W["vn_g"].shape[1]
    H = D // RW_HEAD_DIM
    lc = min(L, CHUNK)
    x = x.reshape(T, D)
    tm = 512
    tm_mix = 256
    chunk_v, conv_new, shift_new, wkv_new = [], [], [], []
    for i in range(depth):
        j = i // 2
        if i % 2 == 0:
            proj = _even_inproj(x, W["g_mix"][i][None], W["w_in"][j], d_a, tm, d_a)
            ws_eff = jnp.tile(W["w_s"][j][:, :lc, :lc], (1, CHUNK // lc, CHUNK // lc))
            bias = jnp.tile(jnp.repeat(W["b_s"][j][:, :lc].T, A_GROUP_DIM, axis=1), (CHUNK // lc, 1))
            st = conv0[j]
            if prompt:
                b1 = b2 = st
            else:
                zero = jnp.zeros((B, L, st.shape[-1]), F32)
                b1 = zero.at[:, 0].set(st[:, 1]).reshape(T, -1)
                b2 = zero.at[:, 0].set(st[:, 0]).at[:, 1].set(st[:, 1]).reshape(T, -1)
            x, v_all, cx_all = _even_mix(proj, x, W["vn_g"][j][None], W["vn_b"][j][None], ws_eff, bias,
                                         W["conv_w"][j], W["w_out"][j], b1, b2, prompt, L, tm_mix, d_a)
            start = ((L - 1) // CHUNK) * CHUNK
            chunk_v.append(v_all.reshape(B, L, -1)[:, start:])
            conv_new.append(cx_all.reshape(B, L, -1)[:, L - 2:])
        else:
            st = shift0[j]
            if prompt:
                bnd = st[:, None, :]
            else:
                bnd = jnp.zeros((B, L, D), F32).at[:, 0].set(st).reshape(T, D)
            xrkv, xw, xa, xg, h_all = _rwkv_prep(x, W["g_mix"][i][None], W["rw_mu"][j], bnd, prompt, L, tm_mix)
            rkv = _bmm(xrkv, W["rw_w_rkv"][j], tm, B, L)
            decay = _lora(xw, W["rw_w1"][j], W["rw_w2"][j], W["rw_w0"][j][None], "tanh", "decay", tm, (B, L))
            a = _lora(xa, W["rw_a1"][j], W["rw_a2"][j], W["rw_a0"][j][None], "none", "sigmoid", tm, (B, L))
            g = _lora(xg, W["rw_g1"][j], W["rw_g2"][j], jnp.zeros((1, D), F32), "sigmoid", "none", tm)
            consts = (_chain_tile(W["rw_k_k"][j], H), _chain_tile(W["rw_k_a"][j], H),
                      _chain_tile(W["rw_r_k"][j].reshape(-1), H),
                      _chain_tile(W["rw_gn_g"][j], H), _chain_tile(W["rw_gn_b"][j], H))
            z, s_t = _wkv_scan(rkv, decay, a, _state_to_chains(wkv0[j], H), consts, min(L, 32))
            x = _rwkv_out(z, g, x, W["rw_w_o"][j], tm_mix, L)
            shift_new.append(h_all.reshape(B, L, D)[:, -1])
            wkv_new.append(_state_from_chains(s_t, B, H))
        h, gates_t = _moe_route(x, W["g_ffn"][i][None], W["moe_r_hi"][i], W["moe_r_lo"][i], W["moe_r_b"][i], tm)
        x = _moe_experts(h, gates_t.T, x, W["moe_w_gate"][i], W["moe_w_up"][i], W["moe_w_down"][i], tm)
        x = _ple(x, p[i].reshape(T, -1), W["ple_g"][i][None], W["ple_w_gate"][i], W["ple_w_proj"][i], tm, 1024)
    y = _final_norm(x, W["g_final"][None], tm).reshape(B, L, D)
    return y, jnp.stack(chunk_v), jnp.stack(conv_new), jnp.stack(shift_new), jnp.stack(wkv_new)


def _pad_rank(w1, w2):
    r = w1.shape[-1]
    rp = -(-r // LANES) * LANES
    w1 = jnp.pad(w1, ((0, 0), (0, 0), (0, rp - r)))
    w2 = jnp.pad(w2, ((0, 0), (0, rp - r), (0, 0)))
    return w1.astype(BF16), w2.astype(BF16)


def kernel(x_prompt, x_sample, state_conv, state_shift, state_wkv, p_prompt, p_sample, g_mix, g_ffn, g_final, w_in_even, vn_g, vn_b, w_s, b_s, conv_w, w_out_even, rw_mu, rw_w_rkv, rw_w0, rw_w1, rw_w2, rw_a0, rw_a1, rw_a2, rw_g1, rw_g2, rw_k_k, rw_k_a, rw_r_k, rw_gn_g, rw_gn_b, rw_w_o, moe_w_gr, moe_b_gr, moe_w_er, moe_b_er, moe_w_gate, moe_w_up, moe_w_down, ple_g, ple_w_gate, ple_w_proj):
    depth, D = g_mix.shape
    r_w = jnp.zeros((depth, 32, D), F32)
    r_w = r_w.at[:, :N_GROUPS].set(jnp.swapaxes(moe_w_gr, 1, 2))
    r_w = r_w.at[:, SUBLANES:SUBLANES + N_EXPERTS].set(jnp.swapaxes(moe_w_er, 1, 2))
    r_hi = r_w.astype(BF16)
    r_lo = (r_w - r_hi.astype(F32)).astype(BF16)
    r_b = jnp.zeros((depth, 32, 1), F32)
    r_b = r_b.at[:, :N_GROUPS, 0].set(moe_b_gr).at[:, SUBLANES:SUBLANES + N_EXPERTS, 0].set(moe_b_er)
    w1, w2 = _pad_rank(rw_w1, rw_w2)
    a1, a2 = _pad_rank(rw_a1, rw_a2)
    g1, g2 = _pad_rank(rw_g1, rw_g2)
    W = dict(g_mix=g_mix, g_ffn=g_ffn, g_final=g_final,
             w_in=w_in_even.astype(BF16), vn_g=vn_g, vn_b=vn_b, w_s=w_s, b_s=b_s, conv_w=conv_w,
             w_out=w_out_even.astype(BF16),
             rw_mu=rw_mu, rw_w_rkv=rw_w_rkv.astype(BF16), rw_w0=rw_w0, rw_w1=w1, rw_w2=w2,
             rw_a0=rw_a0, rw_a1=a1, rw_a2=a2, rw_g1=g1, rw_g2=g2,
             rw_k_k=rw_k_k, rw_k_a=rw_k_a, rw_r_k=rw_r_k, rw_gn_g=rw_gn_g, rw_gn_b=rw_gn_b,
             rw_w_o=rw_w_o.astype(BF16),
             moe_r_hi=r_hi, moe_r_lo=r_lo, moe_r_b=r_b,
             moe_w_gate=moe_w_gate.astype(BF16), moe_w_up=moe_w_up.astype(BF16),
             moe_w_down=moe_w_down.astype(BF16),
             ple_g=ple_g, ple_w_gate=ple_w_gate.astype(BF16), ple_w_proj=ple_w_proj.astype(BF16))
    bp = x_prompt.shape[0]
    n_even, n_odd = state_conv.shape[0], state_shift.shape[0]
    conv0 = jnp.zeros((n_even, bp) + state_conv.shape[2:], F32)
    shift0 = jnp.zeros((n_odd, bp, D), F32)
    wkv0 = jnp.zeros((n_odd, bp) + state_wkv.shape[2:], F32)
    y_p, cv_p, conv_p, shift_p, wkv_p = _trunk(x_prompt, p_prompt, conv0, shift0, wkv0, W, True)
    y_s, cv_s, conv_s, shift_s, wkv_s = _trunk(x_sample, p_sample, state_conv, state_shift, state_wkv, W, False)
    return (y_p, y_s, cv_p, conv_p, shift_p, wkv_p, cv_s, conv_s, shift_s, wkv_s)
```

```python
import functools

import jax
import jax.numpy as jnp
from jax import lax
from jax.experimental import pallas as pl
from jax.experimental.pallas import tpu as pltpu

F32 = jnp.float32
BF16 = jnp.bfloat16

NORM_EPS = 1e-6
LN_EPS = 1e-5
RW_GN_EPS = 64e-5
CHUNK = 128
A_GROUP_DIM = 128
RW_HEAD_DIM = 64
N_GROUPS = 4
EXP_PER_GROUP = 4
N_EXPERTS = N_GROUPS * EXP_PER_GROUP

LANES = 128
SUBLANES = 8
VMEM_LIMIT = 52 * 1024 * 1024


def _cparams(*sem):
    return pltpu.CompilerParams(dimension_semantics=sem, vmem_limit_bytes=VMEM_LIMIT)


def _rms(x, g):
    return x * lax.rsqrt(jnp.mean(x * x, axis=-1, keepdims=True) + NORM_EPS) * g


def _gelu_tanh(x):
    return 0.5 * x * (1.0 + jnp.tanh(0.7978845608028654 * (x + 0.044715 * (x * x * x))))


def _sigmoid(x):
    return 1.0 / (1.0 + jnp.exp(-x))


def _dot(a, b):
    return jnp.dot(a, b, preferred_element_type=F32)


def _even_inproj_kernel(n_gelu, x_ref, g_ref, w_ref, o_ref, h_ref):
    n = pl.program_id(1)

    @pl.when(n == 0)
    def _():
        h_ref[...] = _rms(x_ref[...], g_ref[...]).astype(BF16)

    acc = _dot(h_ref[...], w_ref[...])

    @pl.when(n < n_gelu)
    def _():
        o_ref[...] = _gelu_tanh(acc)

    @pl.when(n >= n_gelu)
    def _():
        o_ref[...] = acc


def _even_inproj(x, g, w, d_a, tm, tn):
    T, D = x.shape
    N = w.shape[1]
    return pl.pallas_call(
        functools.partial(_even_inproj_kernel, (2 * d_a) // tn),
        grid=(T // tm, N // tn),
        in_specs=[pl.BlockSpec((tm, D), lambda i, n: (i, 0)),
                  pl.BlockSpec((1, D), lambda i, n: (0, 0)),
                  pl.BlockSpec((D, tn), lambda i, n: (0, n))],
        out_specs=pl.BlockSpec((tm, tn), lambda i, n: (i, n)),
        out_shape=jax.ShapeDtypeStruct((T, N), F32),
        scratch_shapes=[pltpu.VMEM((tm, D), BF16)],
        compiler_params=_cparams("arbitrary", "arbitrary"),
        name="even_inproj",
    )(x, g, w)


def _even_mix_kernel(prompt, seq_len, tiles_per_seq, tm, d_a,
                     u_ref, v_ref, xi_ref, gb_ref, gc_ref, x_ref, vng_ref, vnb_ref,
                     ws_ref, bias_ref, cw_ref, wout_ref, b1_ref, b2_ref,
                     xo_ref, vo_ref, cxo_ref, y_ref, carry_ref):
    i = pl.program_id(0)
    lc = min(seq_len, CHUNK)

    vg = v_ref[...]
    mu = jnp.mean(vg, axis=-1, keepdims=True)
    vc = vg - mu
    var = jnp.mean(vc * vc, axis=-1, keepdims=True)
    vln = vc * lax.rsqrt(var + LN_EPS) * vng_ref[...] + vnb_ref[...]
    vo_ref[...] = vln

    r = lax.broadcasted_iota(jnp.int32, (CHUNK, CHUNK), 0)
    c = lax.broadcasted_iota(jnp.int32, (CHUNK, CHUNK), 1)
    keep = (r >= c) & ((r // lc) == (c // lc))
    n_groups = d_a // A_GROUP_DIM
    for g in range(n_groups):
        wg = jnp.where(keep, ws_ref[g], 0.0).astype(BF16)
        cols = slice(g * A_GROUP_DIM, (g + 1) * A_GROUP_DIM)
        for ch in range(tm // CHUNK):
            rows = slice(ch * CHUNK, (ch + 1) * CHUNK)
            mixed = _dot(wg, vln[rows, cols].astype(BF16)) + bias_ref[:, cols]
            y_ref[rows, cols] = (u_ref[rows, cols] * mixed).astype(BF16)

    cx = gc_ref[...] * xi_ref[...]
    cxo_ref[...] = cx
    row = lax.broadcasted_iota(jnp.int32, cx.shape, 0)
    r1 = pltpu.roll(cx, 1, 0)
    r2 = pltpu.roll(cx, 2, 0)
    if prompt:
        @pl.when(i % tiles_per_seq == 0)
        def _():
            carry_ref[...] = jnp.zeros_like(carry_ref)
            carry_ref[SUBLANES - 2:SUBLANES, :] = b1_ref[0]
        p1 = carry_ref[SUBLANES - 1:SUBLANES, :]
        p2 = carry_ref[SUBLANES - 2:SUBLANES - 1, :]
        s1 = jnp.where(row == 0, p1, r1)
        s2 = jnp.where(row == 0, p2, jnp.where(row == 1, p1, r2))
        carry_ref[...] = cx[tm - SUBLANES:tm, :]
    else:
        pos = row % seq_len
        s1 = jnp.where(pos == 0, b1_ref[...], r1)
        s2 = jnp.where(pos <= 1, b2_ref[...], r2)
    conv = s2 * cw_ref[0:1, :] + s1 * cw_ref[1:2, :] + cx * cw_ref[2:3, :]
    y_ref[:, d_a:] = (gb_ref[...] * conv).astype(BF16)

    xo_ref[...] = x_ref[...] + _dot(y_ref[...], wout_ref[...])


def _even_mix(proj, x, vn_g, vn_b, ws_eff, bias, conv_w, w_out, b1, b2, prompt, seq_len, tm, d_a):
    T, D = x.shape
    d_b = D - d_a
    tps = max(seq_len // tm, 1)
    col = lambda k: pl.BlockSpec((tm, d_a), lambda i, k=k: (i, k))
    const = lambda shape: pl.BlockSpec(shape, lambda i: (0,) * len(shape))
    if prompt:
        b1_spec = pl.BlockSpec((1, 2, d_b), lambda i: (i // tps, 0, 0))
        b2_spec = pl.BlockSpec((1, 2, d_b), lambda i: (i // tps, 0, 0))
    else:
        b1_spec = pl.BlockSpec((tm, d_b), lambda i: (i, 0))
        b2_spec = pl.BlockSpec((tm, d_b), lambda i: (i, 0))
    return pl.pallas_call(
        functools.partial(_even_mix_kernel, prompt, seq_len, tps, tm, d_a),
        grid=(T // tm,),
        in_specs=[col(0), col(1), col(2), col(3), col(4),
                  pl.BlockSpec((tm, D), lambda i: (i, 0)),
                  const((1, d_a)), const((1, d_a)),
                  const(ws_eff.shape), const(bias.shape), const(conv_w.shape), const(w_out.shape),
                  b1_spec, b2_spec],
        out_specs=[pl.BlockSpec((tm, D), lambda i: (i, 0)),
                   pl.BlockSpec((tm, d_a), lambda i: (i, 0)),
                   pl.BlockSpec((tm, d_b), lambda i: (i, 0))],
        out_shape=[jax.ShapeDtypeStruct((T, D), F32),
                   jax.ShapeDtypeStruct((T, d_a), F32),
                   jax.ShapeDtypeStruct((T, d_b), F32)],
        scratch_shapes=[pltpu.VMEM((tm, D), BF16), pltpu.VMEM((SUBLANES, d_b), F32)],
        compiler_params=_cparams("arbitrary"),
        name="even_mix",
    )(proj, proj, proj, proj, proj, x, vn_g, vn_b, ws_eff, bias, conv_w, w_out, b1, b2)


def _rwkv_prep_kernel(prompt, seq_len, tiles_per_seq, tm,
                      x_ref, g_ref, mu_ref, b_ref,
                      rkv_ref, xw_ref, xa_ref, xg_ref, h_ref, carry_ref):
    i = pl.program_id(0)
    h = _rms(x_ref[...], g_ref[...])
    h_ref[...] = h
    row = lax.broadcasted_iota(jnp.int32, h.shape, 0)
    r1 = pltpu.roll(h, 1, 0)
    if prompt:
        @pl.when(i % tiles_per_seq == 0)
        def _():
            carry_ref[...] = jnp.zeros_like(carry_ref)
            carry_ref[SUBLANES - 1:SUBLANES, :] = b_ref[0]
        prev = jnp.where(row == 0, carry_ref[SUBLANES - 1:SUBLANES, :], r1)
        carry_ref[...] = h[tm - SUBLANES:tm, :]
    else:
        prev = jnp.where(row % seq_len == 0, b_ref[...], r1)
    xx = prev - h
    rkv_ref[0] = (h + xx * mu_ref[0:1, :]).astype(BF16)
    xw_ref[...] = (h + xx * mu_ref[1:2, :]).astype(BF16)
    rkv_ref[1] = (h + xx * mu_ref[2:3, :]).astype(BF16)
    rkv_ref[2] = (h + xx * mu_ref[3:4, :]).astype(BF16)
    xa_ref[...] = (h + xx * mu_ref[4:5, :]).astype(BF16)
    xg_ref[...] = (h + xx * mu_ref[5:6, :]).astype(BF16)


def _rwkv_prep(x, g, mu, bnd, prompt, seq_len, tm):
    T, D = x.shape
    tps = max(seq_len // tm, 1)
    if prompt:
        b_spec = pl.BlockSpec((1, 1, D), lambda i: (i // tps, 0, 0))
    else:
        b_spec = pl.BlockSpec((tm, D), lambda i: (i, 0))
    tok = pl.BlockSpec((tm, D), lambda i: (i, 0))
    return pl.pallas_call(
        functools.partial(_rwkv_prep_kernel, prompt, seq_len, tps, tm),
        grid=(T // tm,),
        in_specs=[tok, pl.BlockSpec((1, D), lambda i: (0, 0)),
                  pl.BlockSpec(mu.shape, lambda i: (0, 0)), b_spec],
        out_specs=[pl.BlockSpec((3, tm, D), lambda i: (0, i, 0)), tok, tok, tok, tok],
        out_shape=[jax.ShapeDtypeStruct((3, T, D), BF16),
                   jax.ShapeDtypeStruct((T, D), BF16),
                   jax.ShapeDtypeStruct((T, D), BF16),
                   jax.ShapeDtypeStruct((T, D), BF16),
                   jax.ShapeDtypeStruct((T, D), F32)],
        scratch_shapes=[pltpu.VMEM((SUBLANES, D), F32)],
        compiler_params=_cparams("arbitrary"),
        name="rwkv_prep",
    )(x, g, mu, bnd)


HEAD_PAIRS = 16
GROUP_BATCH = 4


def _store_scan_rows(o_ref, lead, acc, nb, lt):
    if nb == 1:
        o_ref[lead + (0,)] = acc.reshape(lt, HEAD_PAIRS, LANES)
    else:
        a5 = acc.reshape(nb // GROUP_BATCH, GROUP_BATCH, lt, HEAD_PAIRS, LANES)
        for bl in range(GROUP_BATCH):
            o_ref[lead + (slice(None), slice(None), slice(bl * HEAD_PAIRS, (bl + 1) * HEAD_PAIRS))] = a5[:, bl]


def _load_scan_rows(z_ref, nb, lt):
    if nb == 1:
        return z_ref[0].reshape(lt, HEAD_PAIRS * LANES)
    parts = [z_ref[:, :, bl * HEAD_PAIRS:(bl + 1) * HEAD_PAIRS, :] for bl in range(GROUP_BATCH)]
    return jnp.stack(parts, axis=1).reshape(nb * lt, HEAD_PAIRS * LANES)


def _scan_rows_spec(nb, lt, tps, lead=()):
    nl = len(lead)
    if nb == 1:
        shape = (1,) * nl + (1, lt, HEAD_PAIRS, LANES)
        return shape, lambda i: (i // tps // GROUP_BATCH, i % tps, (i // tps) % GROUP_BATCH, 0)
    shape = (1,) * nl + (nb // GROUP_BATCH, lt, GROUP_BATCH * HEAD_PAIRS, LANES)
    return shape, lambda i: (i, 0, 0, 0)


def _tile_geometry(tm, seq_len):
    lt = min(tm, seq_len)
    return tm // lt, lt, seq_len // lt


def _bmm_kernel(nb, lt, x_ref, w_ref, o_ref):
    _store_scan_rows(o_ref, (0,), _dot(x_ref[0], w_ref[0]), nb, lt)


def _bmm(x, w, tm, B, L):
    n3, T, D = x.shape
    N = w.shape[2]
    nb, lt, tps = _tile_geometry(tm, L)
    shape, idx = _scan_rows_spec(nb, lt, tps, lead=(0,))
    return pl.pallas_call(
        functools.partial(_bmm_kernel, nb, lt),
        grid=(n3, T // tm),
        in_specs=[pl.BlockSpec((1, tm, D), lambda b, i: (b, i, 0)),
                  pl.BlockSpec((1, D, N), lambda b, i: (b, 0, 0))],
        out_specs=pl.BlockSpec(shape, lambda b, i: (b,) + idx(i)),
        out_shape=jax.ShapeDtypeStruct((n3, B // GROUP_BATCH, L, GROUP_BATCH * HEAD_PAIRS, LANES), F32),
        compiler_params=_cparams("arbitrary", "arbitrary"),
        name="rwkv_rkv_proj",
    )(x, w)


def _lora_kernel(mid, out, geom, x_ref, w1_ref, w2_ref, b_ref, o_ref):
    t = _dot(x_ref[...], w1_ref[...])
    if mid == "tanh":
        t = jnp.tanh(t)
    elif mid == "sigmoid":
        t = _sigmoid(t)
    z = _dot(t.astype(BF16), w2_ref[...]) + b_ref[...]
    if out == "decay":
        w_log = jnp.minimum(z, 0.0) - jnp.log(1.0 + jnp.exp(-jnp.abs(z))) - 0.5
        z = jnp.exp(-jnp.exp(w_log))
    elif out == "sigmoid":
        z = _sigmoid(z)
    if geom is None:
        o_ref[...] = z
    else:
        _store_scan_rows(o_ref, (), z, *geom)


def _lora(x, w1, w2, b, mid, out, tm, head_pairs=None):
    T, D = x.shape
    R = w1.shape[1]
    if head_pairs is None:
        geom = None
        out_spec = pl.BlockSpec((tm, D), lambda i: (i, 0))
        out_shape = jax.ShapeDtypeStruct((T, D), F32)
    else:
        B, L = head_pairs
        nb, lt, tps = _tile_geometry(tm, L)
        geom = (nb, lt)
        out_spec = pl.BlockSpec(*_scan_rows_spec(nb, lt, tps))
        out_shape = jax.ShapeDtypeStruct((B // GROUP_BATCH, L, GROUP_BATCH * HEAD_PAIRS, LANES), F32)
    return pl.pallas_call(
        functools.partial(_lora_kernel, mid, out, geom),
        grid=(T // tm,),
        in_specs=[pl.BlockSpec((tm, D), lambda i: (i, 0)),
                  pl.BlockSpec((D, R), lambda i: (0, 0)),
                  pl.BlockSpec((R, D), lambda i: (0, 0)),
                  pl.BlockSpec((1, D), lambda i: (0, 0))],
        out_specs=out_spec,
        out_shape=out_shape,
        compiler_params=_cparams("arbitrary"),
        name="rwkv_lora_" + out,
    )(x, w1, w2, b)


def _wkv_scan_kernel(tb, r_ref, k_ref, v_ref, w_ref, a_ref, s0_ref,
                     kk_c, ka_c, rk_c, gg_c, gb_c, z_ref, st_ref, s_ref, tile_ref, vec_ref, zbuf_ref):
    n = RW_HEAD_DIM
    half = LANES // 2
    step0 = pl.program_id(1)
    seq_refs = (r_ref, w_ref, k_ref, v_ref, a_ref)
    R, W, K, V, A = range(5)

    @pl.when(step0 == 0)
    def _():
        zbuf_ref[...] = jnp.zeros_like(zbuf_ref)
        for c in range(n * n // LANES):
            xt = s0_ref[:, c * LANES:(c + 1) * LANES].T
            for il in range(LANES // n):
                s_ref[pl.ds(c * (LANES // n) + il, n, stride=n), :] = xt[il * n:(il + 1) * n]

    def fetch(t, slot):
        for q, ref in enumerate(seq_refs):
            xt = ref[(0,) * (len(ref.shape) - 3) + (t,)].T
            tile_ref[slot, q] = jnp.concatenate([xt[:n], xt[n:]], axis=1)

    def flush(t, slot):
        z = zbuf_ref[slot]
        z_ref[0, t] = jnp.concatenate([z[:, :half], z[:, half:]], axis=0).T

    def recur(slot):
        k_t = tile_ref[slot, K]
        a_t = tile_ref[slot, A]
        v_t = tile_ref[slot, V]
        kk = k_t * kk_c[...]
        nrm = jnp.sqrt(jnp.sum(kk * kk, axis=0, keepdims=True))
        kk = kk / jnp.maximum(nrm, 1e-12)
        kmod = k_t * (1.0 + (a_t - 1.0) * ka_c[...])
        vec_ref[slot, 0] = kk
        vec_ref[slot, 1] = kk * a_t
        vec_ref[slot, 2] = kmod

        sa = jnp.zeros((n, LANES), F32)
        for j in range(n):
            sa = sa + s_ref[j * n:(j + 1) * n, :] * vec_ref[slot, 0, j:j + 1, :]
        y = jnp.zeros((n, LANES), F32)
        for j in range(n):
            s_new = (s_ref[j * n:(j + 1) * n, :] * tile_ref[slot, W, j:j + 1, :]
                     - sa * vec_ref[slot, 1, j:j + 1, :] + v_t * vec_ref[slot, 2, j:j + 1, :])
            s_ref[j * n:(j + 1) * n, :] = s_new
            y = y + s_new * tile_ref[slot, R, j:j + 1, :]

        mu = jnp.mean(y, axis=0, keepdims=True)
        yc = y - mu
        var = jnp.mean(yc * yc, axis=0, keepdims=True)
        yn = yc * lax.rsqrt(var + RW_GN_EPS) * gg_c[...] + gb_c[...]
        bonus = jnp.sum(tile_ref[slot, R] * kmod * rk_c[...], axis=0, keepdims=True) * v_t
        zbuf_ref[slot] = yn + bonus

    fetch(0, 0)

    def pair(u, carry):
        t = 2 * u
        fetch(t + 1, 1)
        flush(jnp.maximum(t - 1, 0), 1)
        recur(0)
        fetch(jnp.minimum(t + 2, tb - 1), 0)
        flush(t, 0)
        recur(1)
        return carry

    lax.fori_loop(0, tb // 2, pair, 0)
    flush(tb - 1, 1)

    @pl.when(step0 == pl.num_programs(1) - 1)
    def _():
        for c in range(n * n // LANES):
            xt = jnp.concatenate([s_ref[pl.ds(c * (LANES // n) + il, n, stride=n), :]
                                  for il in range(LANES // n)], axis=0)
            st_ref[:, c * LANES:(c + 1) * LANES] = xt.T


def _wkv_scan(rkv, w, a, s0, consts, tb):
    _, G, L, half, _ = rkv.shape
    n = RW_HEAD_DIM
    assert tb % 2 == 0 and L % tb == 0 and half == LANES // 2
    seq3 = lambda m: pl.BlockSpec((1, 1, tb, half, LANES), lambda g, t, m=m: (m, g, t, 0, 0))
    seq = pl.BlockSpec((1, tb, half, LANES), lambda g, t: (g, t, 0, 0))
    st = pl.BlockSpec((LANES, n * n), lambda g, t: (g, 0))
    cst = pl.BlockSpec((n, LANES), lambda g, t: (0, 0))
    return pl.pallas_call(
        functools.partial(_wkv_scan_kernel, tb),
        grid=(G, L // tb),
        in_specs=[seq3(0), seq3(1), seq3(2), seq, seq, st, cst, cst, cst, cst, cst],
        out_specs=[seq, st],
        out_shape=[jax.ShapeDtypeStruct((G, L, half, LANES), F32),
                   jax.ShapeDtypeStruct((G * LANES, n * n), F32)],
        scratch_shapes=[pltpu.VMEM((n * n, LANES), F32), pltpu.VMEM((2, 5, n, LANES), F32),
                        pltpu.VMEM((2, 3, n, LANES), F32), pltpu.VMEM((2, n, LANES), F32)],
        compiler_params=_cparams("arbitrary", "arbitrary"),
        name="rwkv_scan",
    )(rkv, rkv, rkv, w, a, s0, *consts)


def _rwkv_out_kernel(nb, lt, z_ref, g_ref, x_ref, w_ref, o_ref):
    z = _load_scan_rows(z_ref, nb, lt)
    o_ref[...] = x_ref[...] + _dot((z * g_ref[...]).astype(BF16), w_ref[...])


def _rwkv_out(z, g, x, w, tm, L):
    T, D = x.shape
    nb, lt, tps = _tile_geometry(tm, L)
    tok = pl.BlockSpec((tm, D), lambda i: (i, 0))
    return pl.pallas_call(
        functools.partial(_rwkv_out_kernel, nb, lt),
        grid=(T // tm,),
        in_specs=[pl.BlockSpec(*_scan_rows_spec(nb, lt, tps)),
                  tok, tok, pl.BlockSpec((D, D), lambda i: (0, 0))],
        out_specs=tok,
        out_shape=jax.ShapeDtypeStruct((T, D), F32),
        compiler_params=_cparams("arbitrary"),
        name="rwkv_out",
    )(z, g, x, w)


def _moe_route_kernel(x_ref, g_ref, whi_ref, wlo_ref, b_ref, info_ref):
    h = _rms(x_ref[...], g_ref[...])
    h_hi = h.astype(BF16)
    h_lo = (h - h_hi.astype(F32)).astype(BF16)
    nt = (((1,), (1,)), ((), ()))
    logits = (lax.dot_general(whi_ref[...], h_hi, nt, preferred_element_type=F32)
              + lax.dot_general(whi_ref[...], h_lo, nt, preferred_element_type=F32)
              + lax.dot_general(wlo_ref[...], h_hi, nt, preferred_element_type=F32)
              + b_ref[...])
    gl = [logits[k:k + 1, :] for k in range(N_GROUPS)]
    gmax = functools.reduce(jnp.maximum, gl)
    sel, taken = [], jnp.zeros_like(gmax)
    for k in range(N_GROUPS):
        s = jnp.where((gl[k] == gmax) & (taken == 0.0), 1.0, 0.0)
        taken = taken + s
        sel.append(s)
    p_group = 1.0 / functools.reduce(jnp.add, [jnp.exp(x - gmax) for x in gl])

    el = []
    for m in range(EXP_PER_GROUP):
        rows = [logits[SUBLANES + k * EXP_PER_GROUP + m:SUBLANES + k * EXP_PER_GROUP + m + 1, :]
                for k in range(N_GROUPS)]
        el.append(functools.reduce(jnp.add, [jnp.where(sel[k] > 0.0, rows[k], 0.0) for k in range(N_GROUPS)]))
    emax = functools.reduce(jnp.maximum, el)
    ee = [jnp.exp(x - emax) for x in el]
    esum = functools.reduce(jnp.add, ee)
    prob = [x / esum for x in ee]

    def first_argmax(vals):
        vmax = functools.reduce(jnp.maximum, vals)
        hot, used = [], jnp.zeros_like(vmax)
        for x in vals:
            s = jnp.where((x == vmax) & (used == 0.0), 1.0, 0.0)
            used = used + s
            hot.append(s)
        return hot, vmax

    t1, p1 = first_argmax(prob)
    rest = [jnp.where(t1[m] > 0.0, -1.0, prob[m]) for m in range(EXP_PER_GROUP)]
    t2, p2 = first_argmax(rest)
    scale = p_group / (p1 + p2)
    group = functools.reduce(jnp.add, [sel[k] * float(k * EXP_PER_GROUP) for k in range(N_GROUPS)])
    e1 = group + functools.reduce(jnp.add, [t1[m] * float(m) for m in range(EXP_PER_GROUP)])
    e2 = group + functools.reduce(jnp.add, [t2[m] * float(m) for m in range(EXP_PER_GROUP)])
    info_ref[...] = jnp.zeros_like(info_ref)
    info_ref[0:1, :] = e1
    info_ref[1:2, :] = e2
    info_ref[2:3, :] = p1 * scale
    info_ref[3:4, :] = p2 * scale


def _moe_route(x, g, w_hi, w_lo, b, tm):
    T, D = x.shape
    R = w_hi.shape[0]
    return pl.pallas_call(
        _moe_route_kernel,
        grid=(T // tm,),
        in_specs=[pl.BlockSpec((tm, D), lambda i: (i, 0)),
                  pl.BlockSpec((1, D), lambda i: (0, 0)),
                  pl.BlockSpec((R, D), lambda i: (0, 0)),
                  pl.BlockSpec((R, D), lambda i: (0, 0)),
                  pl.BlockSpec((R, 1), lambda i: (0, 0))],
        out_specs=pl.BlockSpec((SUBLANES, tm), lambda i: (0, i)),
        out_shape=jax.ShapeDtypeStruct((SUBLANES, T), F32),
        compiler_params=_cparams("arbitrary"),
        name="moe_route",
    )(x, g, w_hi, w_lo, b)


def _moe_plan(info, tr):
    T = info.shape[1]
    P = 2 * T
    n_tiles = P // tr + N_EXPERTS
    eid = info[:2].astype(jnp.int32).reshape(P)
    tok = jnp.tile(jnp.arange(T, dtype=jnp.int32), 2)
    onehot = (eid[:, None] == jnp.arange(N_EXPERTS, dtype=jnp.int32)[None, :]).astype(jnp.int32)
    rank = jnp.sum((jnp.cumsum(onehot, axis=0) - onehot) * onehot, axis=1)
    counts = jnp.sum(onehot, axis=0)
    padded = ((counts + tr - 1) // tr) * tr
    ends = jnp.cumsum(padded)
    slot = (ends - padded)[eid] + rank
    src_tok = jnp.zeros((n_tiles * tr,), jnp.int32).at[slot].set(tok)
    n_used = ends[-1] // tr
    tile_expert = jnp.searchsorted(ends, jnp.arange(n_tiles, dtype=jnp.int32) * tr, side="right").astype(jnp.int32)
    tile_expert = jnp.minimum(tile_expert, tile_expert[jnp.maximum(n_used - 1, 0)])
    return src_tok, tile_expert, n_used.reshape(1).astype(jnp.int32), slot.astype(jnp.int32)


def _moe_expert_kernel(tr, src_ref, texp_ref, nused_ref, x_hbm, g_ref, wg_ref, wu_ref, wd_ref, o_ref,
                       xbuf, wg_b, wu_b, wd_b, sem):
    i = pl.program_id(0)
    n_used = nused_ref[0]

    def start_gather(tile, buf):
        def body(r, c):
            tok = src_ref[tile * tr + r]
            pltpu.make_async_copy(x_hbm.at[pl.ds(tok, 1)], xbuf.at[buf, pl.ds(r, 1)], sem.at[buf]).start()
            return c
        lax.fori_loop(0, tr, body, 0)

    def wait_gather(buf):
        pltpu.make_async_copy(x_hbm.at[pl.ds(0, tr)], xbuf.at[buf], sem.at[buf]).wait()

    @pl.when(i == 0)
    def _():
        start_gather(0, 0)

    @pl.when(i + 1 < n_used)
    def _():
        start_gather(i + 1, (i + 1) % 2)

    @pl.when((i == 0) | (texp_ref[i] != texp_ref[jnp.maximum(i - 1, 0)]))
    def _():
        wg_b[...] = wg_ref[0].astype(BF16)
        wu_b[...] = wu_ref[0].astype(BF16)
        wd_b[...] = wd_ref[0].astype(BF16)

    @pl.when(i < n_used)
    def _():
        buf = i % 2
        wait_gather(buf)
        h = _rms(xbuf[buf], g_ref[...]).astype(BF16)
        hg = _dot(h, wg_b[...])
        hu = _dot(h, wu_b[...])
        act = hg * _sigmoid(hg) * hu
        o_ref[...] = _dot(act.astype(BF16), wd_b[...])

    @pl.when(i >= n_used)
    def _():
        o_ref[...] = jnp.zeros_like(o_ref)


def _moe_experts(x, g, src_tok, tile_expert, n_used, w_gate, w_up, w_down, tr):
    T, D = x.shape
    E, _, Fd = w_gate.shape
    n_tiles = src_tok.shape[0] // tr
    grid_spec = pltpu.PrefetchScalarGridSpec(
        num_scalar_prefetch=3,
        grid=(n_tiles,),
        in_specs=[pl.BlockSpec(memory_space=pl.ANY),
                  pl.BlockSpec((1, D), lambda i, s, e, n: (0, 0)),
                  pl.BlockSpec((1, D, Fd), lambda i, s, e, n: (e[i], 0, 0)),
                  pl.BlockSpec((1, D, Fd), lambda i, s, e, n: (e[i], 0, 0)),
                  pl.BlockSpec((1, Fd, D), lambda i, s, e, n: (e[i], 0, 0))],
        out_specs=pl.BlockSpec((tr, D), lambda i, s, e, n: (i, 0)),
        scratch_shapes=[pltpu.VMEM((2, tr, D), F32), pltpu.VMEM((D, Fd), BF16), pltpu.VMEM((D, Fd), BF16),
                        pltpu.VMEM((Fd, D), BF16), pltpu.SemaphoreType.DMA((2,))],
    )
    return pl.pallas_call(
        functools.partial(_moe_expert_kernel, tr),
        grid_spec=grid_spec,
        out_shape=jax.ShapeDtypeStruct((n_tiles * tr, D), F32),
        compiler_params=_cparams("arbitrary"),
        name="moe_experts",
    )(src_tok, tile_expert, n_used, x, g, w_gate, w_up, w_down)


def _moe_combine_kernel(tm, slot_ref, info_ref, x_ref, ys_hbm, o_ref, ybuf, sem):
    i = pl.program_id(0)
    n = pl.num_programs(0)
    T = n * tm

    def start_gather(tile, buf):
        def body(r, c):
            for k in range(2):
                s = slot_ref[k * T + tile * tm + r]
                pltpu.make_async_copy(ys_hbm.at[pl.ds(s, 1)], ybuf.at[buf, k, pl.ds(r, 1)], sem.at[buf]).start()
            return c
        lax.fori_loop(0, tm, body, 0)

    def wait_gather(buf):
        for k in range(2):
            pltpu.make_async_copy(ys_hbm.at[pl.ds(0, tm)], ybuf.at[buf, k], sem.at[buf]).wait()

    @pl.when(i == 0)
    def _():
        start_gather(0, 0)

    @pl.when(i + 1 < n)
    def _():
        start_gather(i + 1, (i + 1) % 2)

    buf = i % 2
    wait_gather(buf)
    gates = info_ref[...].T
    o_ref[...] = x_ref[...] + gates[:, 2:3] * ybuf[buf, 0] + gates[:, 3:4] * ybuf[buf, 1]


def _moe_combine(x, info, slot, ys, tm):
    T, D = x.shape
    grid_spec = pltpu.PrefetchScalarGridSpec(
        num_scalar_prefetch=1,
        grid=(T // tm,),
        in_specs=[pl.BlockSpec((SUBLANES, tm), lambda i, s: (0, i)),
                  pl.BlockSpec((tm, D), lambda i, s: (i, 0)),
                  pl.BlockSpec(memory_space=pl.ANY)],
        out_specs=pl.BlockSpec((tm, D), lambda i, s: (i, 0)),
        scratch_shapes=[pltpu.VMEM((2, 2, tm, D), F32), pltpu.SemaphoreType.DMA((2,))],
    )
    return pl.pallas_call(
        functools.partial(_moe_combine_kernel, tm),
        grid_spec=grid_spec,
        out_shape=jax.ShapeDtypeStruct((T, D), F32),
        compiler_params=_cparams("arbitrary"),
        name="moe_combine",
    )(slot, info, x, ys)


def _ple_kernel(x_ref, xr_ref, p_ref, g_ref, wg_ref, wp_ref, o_ref, h_ref):
    @pl.when(pl.program_id(1) == 0)
    def _():
        h_ref[...] = _rms(x_ref[...], g_ref[...]).astype(BF16)

    gate = _sigmoid(_dot(h_ref[...], wg_ref[...]))
    o_ref[...] = xr_ref[...] + gate * _dot(p_ref[...].astype(BF16), wp_ref[...])


def _ple(x, p, g, w_gate, w_proj, tm, tn):
    T, D = x.shape
    P = p.shape[1]
    return pl.pallas_call(
        _ple_kernel,
        grid=(T // tm, D // tn),
        in_specs=[pl.BlockSpec((tm, D), lambda i, n: (i, 0)),
                  pl.BlockSpec((tm, tn), lambda i, n: (i, n)),
                  pl.BlockSpec((tm, P), lambda i, n: (i, 0)),
                  pl.BlockSpec((1, D), lambda i, n: (0, 0)),
                  pl.BlockSpec((D, tn), lambda i, n: (0, n)),
                  pl.BlockSpec((P, tn), lambda i, n: (0, n))],
        out_specs=pl.BlockSpec((tm, tn), lambda i, n: (i, n)),
        out_shape=jax.ShapeDtypeStruct((T, D), F32),
        scratch_shapes=[pltpu.VMEM((tm, D), BF16)],
        compiler_params=_cparams("arbitrary", "arbitrary"),
        name="ple",
    )(x, x, p, g, w_gate, w_proj)


def _final_norm_kernel(x_ref, g_ref, o_ref):
    o_ref[...] = _rms(x_ref[...], g_ref[...])


def _final_norm(x, g, tm):
    T, D = x.shape
    return pl.pallas_call(
        _final_norm_kernel,
        grid=(T // tm,),
        in_specs=[pl.BlockSpec((tm, D), lambda i: (i, 0)), pl.BlockSpec((1, D), lambda i: (0, 0))],
        out_specs=pl.BlockSpec((tm, D), lambda i: (i, 0)),
        out_shape=jax.ShapeDtypeStruct((T, D), F32),
        compiler_params=_cparams("arbitrary"),
        name="final_norm",
    )(x, g)


def _chain_tile(p, H):
    t = p.reshape(H // 2, 2, RW_HEAD_DIM).transpose(2, 1, 0)
    bg = LANES // H
    return jnp.broadcast_to(t[:, :, None, :], (RW_HEAD_DIM, 2, bg, H // 2)).reshape(RW_HEAD_DIM, LANES)


def _state_to_chains(s, H):
    B = s.shape[0]
    bg = LANES // H
    s = s.reshape(B // bg, bg, H // 2, 2, RW_HEAD_DIM * RW_HEAD_DIM)
    return s.transpose(0, 3, 1, 2, 4).reshape(B * H, RW_HEAD_DIM * RW_HEAD_DIM)


def _state_from_chains(s, B, H):
    bg = LANES // H
    s = s.reshape(B // bg, 2, bg, H // 2, RW_HEAD_DIM, RW_HEAD_DIM)
    return s.transpose(0, 2, 3, 1, 4, 5).reshape(B, H, RW_HEAD_DIM, RW_HEAD_DIM)


def _trunk(x, p, conv0, shift0, wkv0, W, prompt):
    B, L, D = x.shape
    depth = p.shape[0]
    T = B * L
    d_a = W["vn_g"].shape[1]
    H = D // RW_HEAD_DIM
    lc = min(L, CHUNK)
    x = x.reshape(T, D)
    tm = 512
    tm_mix = 256
    tr = 256 if T >= 4096 else 128
    chunk_v, conv_new, shift_new, wkv_new = [], [], [], []
    for i in range(depth):
        j = i // 2
        if i % 2 == 0:
            proj = _even_inproj(x, W["g_mix"][i][None], W["w_in"][j], d_a, tm, d_a)
            ws_eff = jnp.tile(W["w_s"][j][:, :lc, :lc], (1, CHUNK // lc, CHUNK // lc))
            bias = jnp.tile(jnp.repeat(W["b_s"][j][:, :lc].T, A_GROUP_DIM, axis=1), (CHUNK // lc, 1))
            st = conv0[j]
            if prompt:
                b1 = b2 = st
            else:
                zero = jnp.zeros((B, L, st.shape[-1]), F32)
                b1 = zero.at[:, 0].set(st[:, 1]).reshape(T, -1)
                b2 = zero.at[:, 0].set(st[:, 0]).at[:, 1].set(st[:, 1]).reshape(T, -1)
            x, v_all, cx_all = _even_mix(proj, x, W["vn_g"][j][None], W["vn_b"][j][None], ws_eff, bias,
                                         W["conv_w"][j], W["w_out"][j], b1, b2, prompt, L, tm_mix, d_a)
            start = ((L - 1) // CHUNK) * CHUNK
            chunk_v.append(v_all.reshape(B, L, -1)[:, start:])
            conv_new.append(cx_all.reshape(B, L, -1)[:, L - 2:])
        else:
            st = shift0[j]
            if prompt:
                bnd = st[:, None, :]
            else:
                bnd = jnp.zeros((B, L, D), F32).at[:, 0].set(st).reshape(T, D)
            xrkv, xw, xa, xg, h_all = _rwkv_prep(x, W["g_mix"][i][None], W["rw_mu"][j], bnd, prompt, L, tm_mix)
            rkv = _bmm(xrkv, W["rw_w_rkv"][j], tm, B, L)
            decay = _lora(xw, W["rw_w1"][j], W["rw_w2"][j], W["rw_w0"][j][None], "tanh", "decay", tm, (B, L))
            a = _lora(xa, W["rw_a1"][j], W["rw_a2"][j], W["rw_a0"][j][None], "none", "sigmoid", tm, (B, L))
            g = _lora(xg, W["rw_g1"][j], W["rw_g2"][j], jnp.zeros((1, D), F32), "sigmoid", "none", tm)
            consts = (_chain_tile(W["rw_k_k"][j], H), _chain_tile(W["rw_k_a"][j], H),
                      _chain_tile(W["rw_r_k"][j].reshape(-1), H),
                      _chain_tile(W["rw_gn_g"][j], H), _chain_tile(W["rw_gn_b"][j], H))
            z, s_t = _wkv_scan(rkv, decay, a, _state_to_chains(wkv0[j], H), consts, min(L, 32))
            x = _rwkv_out(z, g, x, W["rw_w_o"][j], tm_mix, L)
            shift_new.append(h_all.reshape(B, L, D)[:, -1])
            wkv_new.append(_state_from_chains(s_t, B, H))
        info = _moe_route(x, W["g_ffn"][i][None], W["moe_r_hi"][i], W["moe_r_lo"][i], W["moe_r_b"][i], tm)
        src_tok, tile_expert, n_used, slot = _moe_plan(info, tr)
        ys = _moe_experts(x, W["g_ffn"][i][None], src_tok, tile_expert, n_used,
                          W["moe_w_gate"][i], W["moe_w_up"][i], W["moe_w_down"][i], tr)
        x = _moe_combine(x, info, slot, ys, tm_mix)
        x = _ple(x, p[i].reshape(T, -1), W["ple_g"][i][None], W["ple_w_gate"][i], W["ple_w_proj"][i], tm, 1024)
    y = _final_norm(x, W["g_final"][None], tm).reshape(B, L, D)
    return y, jnp.stack(chunk_v), jnp.stack(conv_new), jnp.stack(shift_new), jnp.stack(wkv_new)


def _pad_rank(w1, w2):
    r = w1.shape[-1]
    rp = -(-r // LANES) * LANES
    w1 = jnp.pad(w1, ((0, 0), (0, 0), (0, rp - r)))
    w2 = jnp.pad(w2, ((0, 0), (0, rp - r), (0, 0)))
    return w1.astype(BF16), w2.astype(BF16)


def kernel(x_prompt, x_sample, state_conv, state_shift, state_wkv, p_prompt, p_sample, g_mix, g_ffn, g_final, w_in_even, vn_g, vn_b, w_s, b_s, conv_w, w_out_even, rw_mu, rw_w_rkv, rw_w0, rw_w1, rw_w2, rw_a0, rw_a1, rw_a2, rw_g1, rw_g2, rw_k_k, rw_k_a, rw_r_k, rw_gn_g, rw_gn_b, rw_w_o, moe_w_gr, moe_b_gr, moe_w_er, moe_b_er, moe_w_gate, moe_w_up, moe_w_down, ple_g, ple_w_gate, ple_w_proj):
    depth, D = g_mix.shape
    r_w = jnp.zeros((depth, 32, D), F32)
    r_w = r_w.at[:, :N_GROUPS].set(jnp.swapaxes(moe_w_gr, 1, 2))
    r_w = r_w.at[:, SUBLANES:SUBLANES + N_EXPERTS].set(jnp.swapaxes(moe_w_er, 1, 2))
    r_hi = r_w.astype(BF16)
    r_lo = (r_w - r_hi.astype(F32)).astype(BF16)
    r_b = jnp.zeros((depth, 32, 1), F32)
    r_b = r_b.at[:, :N_GROUPS, 0].set(moe_b_gr).at[:, SUBLANES:SUBLANES + N_EXPERTS, 0].set(moe_b_er)
    w1, w2 = _pad_rank(rw_w1, rw_w2)
    a1, a2 = _pad_rank(rw_a1, rw_a2)
    g1, g2 = _pad_rank(rw_g1, rw_g2)
    W = dict(g_mix=g_mix, g_ffn=g_ffn, g_final=g_final,
             w_in=w_in_even.astype(BF16), vn_g=vn_g, vn_b=vn_b, w_s=w_s, b_s=b_s, conv_w=conv_w,
             w_out=w_out_even.astype(BF16),
             rw_mu=rw_mu, rw_w_rkv=rw_w_rkv.astype(BF16), rw_w0=rw_w0, rw_w1=w1, rw_w2=w2,
             rw_a0=rw_a0, rw_a1=a1, rw_a2=a2, rw_g1=g1, rw_g2=g2,
             rw_k_k=rw_k_k, rw_k_a=rw_k_a, rw_r_k=rw_r_k, rw_gn_g=rw_gn_g, rw_gn_b=rw_gn_b,
             rw_w_o=rw_w_o.astype(BF16),
             moe_r_hi=r_hi, moe_r_lo=r_lo, moe_r_b=r_b,
             moe_w_gate=moe_w_gate, moe_w_up=moe_w_up, moe_w_down=moe_w_down,
             ple_g=ple_g, ple_w_gate=ple_w_gate.astype(BF16), ple_w_proj=ple_w_proj.astype(BF16))
    bp = x_prompt.shape[0]
    n_even, n_odd = state_conv.shape[0], state_shift.shape[0]
    conv0 = jnp.zeros((n_even, bp) + state_conv.shape[2:], F32)
    shift0 = jnp.zeros((n_odd, bp, D), F32)
    wkv0 = jnp.zeros((n_odd, bp) + state_wkv.shape[2:], F32)
    y_p, cv_p, conv_p, shift_p, wkv_p = _trunk(x_prompt, p_prompt, conv0, shift0, wkv0, W, True)
    y_s, cv_s, conv_s, shift_s, wkv_s = _trunk(x_sample, p_sample, state_conv, state_shift, state_wkv, W, False)
    return (y_p, y_s, cv_p, conv_p, shift_p, wkv_p, cv_s, conv_s, shift_s, wkv_s)
```

```python
import functools

import jax
import jax.numpy as jnp
from jax import lax
from jax.experimental import pallas as pl
from jax.experimental.pallas import tpu as pltpu

F32 = jnp.float32
BF16 = jnp.bfloat16

NORM_EPS = 1e-6
LN_EPS = 1e-5
RW_GN_EPS = 64e-5
CHUNK = 128
A_GROUP_DIM = 128
RW_HEAD_DIM = 64
N_GROUPS = 4
EXP_PER_GROUP = 4
N_EXPERTS = N_GROUPS * EXP_PER_GROUP

LANES = 128
SUBLANES = 8
VMEM_LIMIT = 52 * 1024 * 1024


def _cparams(*sem):
    return pltpu.CompilerParams(dimension_semantics=sem, vmem_limit_bytes=VMEM_LIMIT)


def _rms(x, g):
    return x * lax.rsqrt(jnp.mean(x * x, axis=-1, keepdims=True) + NORM_EPS) * g


def _gelu_tanh(x):
    return 0.5 * x * (1.0 + jnp.tanh(0.7978845608028654 * (x + 0.044715 * (x * x * x))))


def _sigmoid(x):
    return 1.0 / (1.0 + jnp.exp(-x))


def _dot(a, b):
    return jnp.dot(a, b, preferred_element_type=F32)


def _even_inproj_kernel(n_gelu, x_ref, g_ref, w_ref, o_ref, h_ref):
    n = pl.program_id(1)

    @pl.when(n == 0)
    def _():
        h_ref[...] = _rms(x_ref[...], g_ref[...]).astype(BF16)

    acc = _dot(h_ref[...], w_ref[...])

    @pl.when(n < n_gelu)
    def _():
        o_ref[...] = _gelu_tanh(acc)

    @pl.when(n >= n_gelu)
    def _():
        o_ref[...] = acc


def _even_inproj(x, g, w, layer, d_a, tm, tn):
    T, D = x.shape
    N = w.shape[2]
    return pl.pallas_call(
        functools.partial(_even_inproj_kernel, (2 * d_a) // tn),
        grid=(T // tm, N // tn),
        in_specs=[pl.BlockSpec((tm, D), lambda i, n: (i, 0)),
                  pl.BlockSpec((1, D), lambda i, n: (0, 0)),
                  pl.BlockSpec((None, D, tn), lambda i, n: (layer, 0, n))],
        out_specs=pl.BlockSpec((tm, tn), lambda i, n: (i, n)),
        out_shape=jax.ShapeDtypeStruct((T, N), F32),
        scratch_shapes=[pltpu.VMEM((tm, D), BF16)],
        compiler_params=_cparams("arbitrary", "arbitrary"),
        name="even_inproj",
    )(x, g, w)


def _even_mix_kernel(prompt, seq_len, tiles_per_seq, tm, d_a,
                     u_ref, v_ref, xi_ref, gb_ref, gc_ref, x_ref, vng_ref, vnb_ref,
                     ws_ref, bias_ref, cw_ref, wout_ref, b1_ref, b2_ref,
                     xo_ref, vo_ref, cxo_ref, y_ref, carry_ref):
    i = pl.program_id(0)
    lc = min(seq_len, CHUNK)

    vg = v_ref[...]
    mu = jnp.mean(vg, axis=-1, keepdims=True)
    vc = vg - mu
    var = jnp.mean(vc * vc, axis=-1, keepdims=True)
    vln = vc * lax.rsqrt(var + LN_EPS) * vng_ref[...] + vnb_ref[...]
    vo_ref[...] = vln

    r = lax.broadcasted_iota(jnp.int32, (CHUNK, CHUNK), 0)
    c = lax.broadcasted_iota(jnp.int32, (CHUNK, CHUNK), 1)
    keep = (r >= c) & ((r // lc) == (c // lc))
    n_groups = d_a // A_GROUP_DIM
    for g in range(n_groups):
        wg = jnp.where(keep, ws_ref[g], 0.0).astype(BF16)
        cols = slice(g * A_GROUP_DIM, (g + 1) * A_GROUP_DIM)
        for ch in range(tm // CHUNK):
            rows = slice(ch * CHUNK, (ch + 1) * CHUNK)
            mixed = _dot(wg, vln[rows, cols].astype(BF16)) + bias_ref[:, cols]
            y_ref[rows, cols] = (u_ref[rows, cols] * mixed).astype(BF16)

    cx = gc_ref[...] * xi_ref[...]
    cxo_ref[...] = cx
    row = lax.broadcasted_iota(jnp.int32, cx.shape, 0)
    r1 = pltpu.roll(cx, 1, 0)
    r2 = pltpu.roll(cx, 2, 0)
    if prompt:
        @pl.when(i % tiles_per_seq == 0)
        def _():
            carry_ref[...] = jnp.zeros_like(carry_ref)
            carry_ref[SUBLANES - 2:SUBLANES, :] = b1_ref[0]
        p1 = carry_ref[SUBLANES - 1:SUBLANES, :]
        p2 = carry_ref[SUBLANES - 2:SUBLANES - 1, :]
        s1 = jnp.where(row == 0, p1, r1)
        s2 = jnp.where(row == 0, p2, jnp.where(row == 1, p1, r2))
        carry_ref[...] = cx[tm - SUBLANES:tm, :]
    else:
        pos = row % seq_len
        s1 = jnp.where(pos == 0, b1_ref[...], r1)
        s2 = jnp.where(pos <= 1, b2_ref[...], r2)
    conv = s2 * cw_ref[0:1, :] + s1 * cw_ref[1:2, :] + cx * cw_ref[2:3, :]
    y_ref[:, d_a:] = (gb_ref[...] * conv).astype(BF16)

    xo_ref[...] = x_ref[...] + _dot(y_ref[...], wout_ref[...])


def _even_mix(proj, x, vn_g, vn_b, ws_eff, bias, conv_w, w_out, layer, b1, b2, prompt, seq_len, tm, d_a):
    T, D = x.shape
    d_b = D - d_a
    tps = max(seq_len // tm, 1)
    col = lambda k: pl.BlockSpec((tm, d_a), lambda i, k=k: (i, k))
    const = lambda shape: pl.BlockSpec(shape, lambda i: (0,) * len(shape))
    if prompt:
        b1_spec = pl.BlockSpec((1, 2, d_b), lambda i: (i // tps, 0, 0))
        b2_spec = pl.BlockSpec((1, 2, d_b), lambda i: (i // tps, 0, 0))
    else:
        b1_spec = pl.BlockSpec((tm, d_b), lambda i: (i, 0))
        b2_spec = pl.BlockSpec((tm, d_b), lambda i: (i, 0))
    return pl.pallas_call(
        functools.partial(_even_mix_kernel, prompt, seq_len, tps, tm, d_a),
        grid=(T // tm,),
        in_specs=[col(0), col(1), col(2), col(3), col(4),
                  pl.BlockSpec((tm, D), lambda i: (i, 0)),
                  const((1, d_a)), const((1, d_a)),
                  const(ws_eff.shape), const(bias.shape), const(conv_w.shape),
                  pl.BlockSpec((None,) + w_out.shape[1:], lambda i: (layer, 0, 0)),
                  b1_spec, b2_spec],
        out_specs=[pl.BlockSpec((tm, D), lambda i: (i, 0)),
                   pl.BlockSpec((tm, d_a), lambda i: (i, 0)),
                   pl.BlockSpec((tm, d_b), lambda i: (i, 0))],
        out_shape=[jax.ShapeDtypeStruct((T, D), F32),
                   jax.ShapeDtypeStruct((T, d_a), F32),
                   jax.ShapeDtypeStruct((T, d_b), F32)],
        scratch_shapes=[pltpu.VMEM((tm, D), BF16), pltpu.VMEM((SUBLANES, d_b), F32)],
        compiler_params=_cparams("arbitrary"),
        name="even_mix",
    )(proj, proj, proj, proj, proj, x, vn_g, vn_b, ws_eff, bias, conv_w, w_out, b1, b2)


def _rwkv_prep_kernel(prompt, seq_len, tiles_per_seq, tm,
                      x_ref, g_ref, mu_ref, b_ref,
                      rkv_ref, xw_ref, xa_ref, xg_ref, h_ref, carry_ref):
    i = pl.program_id(0)
    h = _rms(x_ref[...], g_ref[...])
    h_ref[...] = h
    row = lax.broadcasted_iota(jnp.int32, h.shape, 0)
    r1 = pltpu.roll(h, 1, 0)
    if prompt:
        @pl.when(i % tiles_per_seq == 0)
        def _():
            carry_ref[...] = jnp.zeros_like(carry_ref)
            carry_ref[SUBLANES - 1:SUBLANES, :] = b_ref[0]
        prev = jnp.where(row == 0, carry_ref[SUBLANES - 1:SUBLANES, :], r1)
        carry_ref[...] = h[tm - SUBLANES:tm, :]
    else:
        prev = jnp.where(row % seq_len == 0, b_ref[...], r1)
    xx = prev - h
    rkv_ref[0] = (h + xx * mu_ref[0:1, :]).astype(BF16)
    xw_ref[...] = (h + xx * mu_ref[1:2, :]).astype(BF16)
    rkv_ref[1] = (h + xx * mu_ref[2:3, :]).astype(BF16)
    rkv_ref[2] = (h + xx * mu_ref[3:4, :]).astype(BF16)
    xa_ref[...] = (h + xx * mu_ref[4:5, :]).astype(BF16)
    xg_ref[...] = (h + xx * mu_ref[5:6, :]).astype(BF16)


def _rwkv_prep(x, g, mu, bnd, prompt, seq_len, tm):
    T, D = x.shape
    tps = max(seq_len // tm, 1)
    if prompt:
        b_spec = pl.BlockSpec((1, 1, D), lambda i: (i // tps, 0, 0))
    else:
        b_spec = pl.BlockSpec((tm, D), lambda i: (i, 0))
    tok = pl.BlockSpec((tm, D), lambda i: (i, 0))
    return pl.pallas_call(
        functools.partial(_rwkv_prep_kernel, prompt, seq_len, tps, tm),
        grid=(T // tm,),
        in_specs=[tok, pl.BlockSpec((1, D), lambda i: (0, 0)),
                  pl.BlockSpec(mu.shape, lambda i: (0, 0)), b_spec],
        out_specs=[pl.BlockSpec((3, tm, D), lambda i: (0, i, 0)), tok, tok, tok, tok],
        out_shape=[jax.ShapeDtypeStruct((3, T, D), BF16),
                   jax.ShapeDtypeStruct((T, D), BF16),
                   jax.ShapeDtypeStruct((T, D), BF16),
                   jax.ShapeDtypeStruct((T, D), BF16),
                   jax.ShapeDtypeStruct((T, D), F32)],
        scratch_shapes=[pltpu.VMEM((SUBLANES, D), F32)],
        compiler_params=_cparams("arbitrary"),
        name="rwkv_prep",
    )(x, g, mu, bnd)


HEAD_PAIRS = 16
GROUP_BATCH = 4


def _store_scan_rows(o_ref, lead, acc, nb, lt):
    if nb == 1:
        o_ref[lead + (0,)] = acc.reshape(lt, HEAD_PAIRS, LANES)
    else:
        a5 = acc.reshape(nb // GROUP_BATCH, GROUP_BATCH, lt, HEAD_PAIRS, LANES)
        for bl in range(GROUP_BATCH):
            o_ref[lead + (slice(None), slice(None), slice(bl * HEAD_PAIRS, (bl + 1) * HEAD_PAIRS))] = a5[:, bl]


def _load_scan_rows(z_ref, nb, lt):
    if nb == 1:
        return z_ref[0].reshape(lt, HEAD_PAIRS * LANES)
    parts = [z_ref[:, :, bl * HEAD_PAIRS:(bl + 1) * HEAD_PAIRS, :] for bl in range(GROUP_BATCH)]
    return jnp.stack(parts, axis=1).reshape(nb * lt, HEAD_PAIRS * LANES)


def _scan_rows_spec(nb, lt, tps, lead=()):
    nl = len(lead)
    if nb == 1:
        shape = (1,) * nl + (1, lt, HEAD_PAIRS, LANES)
        return shape, lambda i: (i // tps // GROUP_BATCH, i % tps, (i // tps) % GROUP_BATCH, 0)
    shape = (1,) * nl + (nb // GROUP_BATCH, lt, GROUP_BATCH * HEAD_PAIRS, LANES)
    return shape, lambda i: (i, 0, 0, 0)


def _tile_geometry(tm, seq_len):
    lt = min(tm, seq_len)
    return tm // lt, lt, seq_len // lt


def _bmm_kernel(nb, lt, x_ref, w_ref, o_ref):
    _store_scan_rows(o_ref, (0,), _dot(x_ref[0], w_ref[0]), nb, lt)


def _bmm(x, w, layer, tm, B, L):
    n3, T, D = x.shape
    N = w.shape[3]
    nb, lt, tps = _tile_geometry(tm, L)
    shape, idx = _scan_rows_spec(nb, lt, tps, lead=(0,))
    return pl.pallas_call(
        functools.partial(_bmm_kernel, nb, lt),
        grid=(n3, T // tm),
        in_specs=[pl.BlockSpec((1, tm, D), lambda b, i: (b, i, 0)),
                  pl.BlockSpec((None, 1, D, N), lambda b, i: (layer, b, 0, 0))],
        out_specs=pl.BlockSpec(shape, lambda b, i: (b,) + idx(i)),
        out_shape=jax.ShapeDtypeStruct((n3, B // GROUP_BATCH, L, GROUP_BATCH * HEAD_PAIRS, LANES), F32),
        compiler_params=_cparams("arbitrary", "arbitrary"),
        name="rwkv_rkv_proj",
    )(x, w)


def _lora_kernel(mid, out, geom, x_ref, w1_ref, w2_ref, b_ref, o_ref):
    t = _dot(x_ref[...], w1_ref[...])
    if mid == "tanh":
        t = jnp.tanh(t)
    elif mid == "sigmoid":
        t = _sigmoid(t)
    z = _dot(t.astype(BF16), w2_ref[...]) + b_ref[...]
    if out == "decay":
        w_log = jnp.minimum(z, 0.0) - jnp.log(1.0 + jnp.exp(-jnp.abs(z))) - 0.5
        z = jnp.exp(-jnp.exp(w_log))
    elif out == "sigmoid":
        z = _sigmoid(z)
    if geom is None:
        o_ref[...] = z
    else:
        _store_scan_rows(o_ref, (), z, *geom)


def _lora(x, w1, w2, b, mid, out, tm, head_pairs=None):
    T, D = x.shape
    R = w1.shape[1]
    if head_pairs is None:
        geom = None
        out_spec = pl.BlockSpec((tm, D), lambda i: (i, 0))
        out_shape = jax.ShapeDtypeStruct((T, D), F32)
    else:
        B, L = head_pairs
        nb, lt, tps = _tile_geometry(tm, L)
        geom = (nb, lt)
        out_spec = pl.BlockSpec(*_scan_rows_spec(nb, lt, tps))
        out_shape = jax.ShapeDtypeStruct((B // GROUP_BATCH, L, GROUP_BATCH * HEAD_PAIRS, LANES), F32)
    return pl.pallas_call(
        functools.partial(_lora_kernel, mid, out, geom),
        grid=(T // tm,),
        in_specs=[pl.BlockSpec((tm, D), lambda i: (i, 0)),
                  pl.BlockSpec((D, R), lambda i: (0, 0)),
                  pl.BlockSpec((R, D), lambda i: (0, 0)),
                  pl.BlockSpec((1, D), lambda i: (0, 0))],
        out_specs=out_spec,
        out_shape=out_shape,
        compiler_params=_cparams("arbitrary"),
        name="rwkv_lora_" + out,
    )(x, w1, w2, b)


def _wkv_scan_kernel(tb, aliased, r_ref, k_ref, v_ref, w_ref, a_ref, s0_ref, kk_c, ka_c, rk_c, gg_c, gb_c, *rest):
    z_ref, st_ref, s_ref, tile_ref, vec_ref, zbuf_ref, slab_ref = rest[1:] if aliased else rest
    n = RW_HEAD_DIM
    half = LANES // 2
    step0 = pl.program_id(1)
    seq_refs = (r_ref, w_ref, k_ref, v_ref, a_ref)
    R, W, K, V, A = range(5)

    nchunk = n * n // LANES
    per_parity = LANES // 2 // GROUP_BATCH
    chain_rows = [pl.ds(b * 2 * per_parity + hl, per_parity, stride=2) for hl in range(2) for b in range(GROUP_BATCH)]

    @pl.when(step0 == 0)
    def _():
        zbuf_ref[...] = jnp.zeros_like(zbuf_ref)
        x2 = s0_ref[...].reshape(LANES, n * n)
        for c in range(nchunk):
            slab_ref[c] = x2[:, c * LANES:(c + 1) * LANES]
        for c in range(nchunk):
            xt = jnp.concatenate([slab_ref[c, rows, :] for rows in chain_rows], axis=0).T
            for il in range(LANES // n):
                s_ref[pl.ds(c * (LANES // n) + il, n, stride=n), :] = xt[il * n:(il + 1) * n]

    def fetch(t, slot):
        for q, ref in enumerate(seq_refs):
            xt = ref[(0,) * (len(ref.shape) - 3) + (t,)].T
            tile_ref[slot, q] = jnp.concatenate([xt[:n], xt[n:]], axis=1)

    def flush(t, slot):
        z = zbuf_ref[slot]
        z_ref[0, t] = jnp.concatenate([z[:, :half], z[:, half:]], axis=0).T

    def recur(slot):
        k_t = tile_ref[slot, K]
        a_t = tile_ref[slot, A]
        v_t = tile_ref[slot, V]
        kk = k_t * kk_c[...]
        nrm = jnp.sqrt(jnp.sum(kk * kk, axis=0, keepdims=True))
        kk = kk / jnp.maximum(nrm, 1e-12)
        kmod = k_t * (1.0 + (a_t - 1.0) * ka_c[...])
        vec_ref[slot, 0] = kk
        vec_ref[slot, 1] = kk * a_t
        vec_ref[slot, 2] = kmod

        sa = jnp.zeros((n, LANES), F32)
        for j in range(n):
            sa = sa + s_ref[j * n:(j + 1) * n, :] * vec_ref[slot, 0, j:j + 1, :]
        y = jnp.zeros((n, LANES), F32)
        for j in range(n):
            s_new = (s_ref[j * n:(j + 1) * n, :] * tile_ref[slot, W, j:j + 1, :]
                     - sa * vec_ref[slot, 1, j:j + 1, :] + v_t * vec_ref[slot, 2, j:j + 1, :])
            s_ref[j * n:(j + 1) * n, :] = s_new
            y = y + s_new * tile_ref[slot, R, j:j + 1, :]

        mu = jnp.mean(y, axis=0, keepdims=True)
        yc = y - mu
        var = jnp.mean(yc * yc, axis=0, keepdims=True)
        yn = yc * lax.rsqrt(var + RW_GN_EPS) * gg_c[...] + gb_c[...]
        bonus = jnp.sum(tile_ref[slot, R] * kmod * rk_c[...], axis=0, keepdims=True) * v_t
        zbuf_ref[slot] = yn + bonus

    fetch(0, 0)

    def pair(u, carry):
        t = 2 * u
        fetch(t + 1, 1)
        flush(jnp.maximum(t - 1, 0), 1)
        recur(0)
        fetch(jnp.minimum(t + 2, tb - 1), 0)
        flush(t, 0)
        recur(1)
        return carry

    lax.fori_loop(0, tb // 2, pair, 0)
    flush(tb - 1, 1)

    @pl.when(step0 == pl.num_programs(1) - 1)
    def _():
        for c in range(nchunk):
            xt = jnp.concatenate([s_ref[pl.ds(c * (LANES // n) + il, n, stride=n), :]
                                  for il in range(LANES // n)], axis=0).T
            for q, rows in enumerate(chain_rows):
                slab_ref[c, rows, :] = xt[q * per_parity:(q + 1) * per_parity]
        x2 = jnp.concatenate([slab_ref[c] for c in range(nchunk)], axis=1)
        st_ref[...] = x2.reshape(st_ref.shape)


def _wkv_scan(rkv, w, a, state, layer, consts, tb, stacked=None):
    _, G, L, half, _ = rkv.shape
    n = RW_HEAD_DIM
    H = state.shape[2]
    assert tb % 2 == 0 and L % tb == 0 and half == LANES // 2 and GROUP_BATCH * H == LANES
    seq3 = lambda m: pl.BlockSpec((1, 1, tb, half, LANES), lambda g, t, m=m: (m, g, t, 0, 0))
    seq = pl.BlockSpec((1, tb, half, LANES), lambda g, t: (g, t, 0, 0))
    st = pl.BlockSpec((None, GROUP_BATCH, H, n, n), lambda g, t: (layer, g, 0, 0, 0))
    cst = pl.BlockSpec((n, LANES), lambda g, t: (0, 0))
    extra, extra_specs, aliases = (), [], {}
    if stacked is not None:
        extra, extra_specs, aliases = (stacked,), [pl.BlockSpec(memory_space=pl.ANY)], {11: 1}
    return pl.pallas_call(
        functools.partial(_wkv_scan_kernel, tb, stacked is not None),
        grid=(G, L // tb),
        in_specs=[seq3(0), seq3(1), seq3(2), seq, seq, st, cst, cst, cst, cst, cst] + extra_specs,
        out_specs=[seq, st],
        out_shape=[jax.ShapeDtypeStruct((G, L, half, LANES), F32),
                   jax.ShapeDtypeStruct(state.shape, F32)],
        scratch_shapes=[pltpu.VMEM((n * n, LANES), F32), pltpu.VMEM((2, 5, n, LANES), F32),
                        pltpu.VMEM((2, 3, n, LANES), F32), pltpu.VMEM((2, n, LANES), F32),
                        pltpu.VMEM((n * n // LANES, LANES, LANES), F32)],
        input_output_aliases=aliases,
        compiler_params=_cparams("arbitrary", "arbitrary"),
        name="rwkv_scan",
    )(rkv, rkv, rkv, w, a, state, *consts, *extra)


def _rwkv_out_kernel(nb, lt, z_ref, g_ref, x_ref, w_ref, o_ref):
    z = _load_scan_rows(z_ref, nb, lt)
    o_ref[...] = x_ref[...] + _dot((z * g_ref[...]).astype(BF16), w_ref[...])


def _rwkv_out(z, g, x, w, layer, tm, L):
    T, D = x.shape
    nb, lt, tps = _tile_geometry(tm, L)
    tok = pl.BlockSpec((tm, D), lambda i: (i, 0))
    return pl.pallas_call(
        functools.partial(_rwkv_out_kernel, nb, lt),
        grid=(T // tm,),
        in_specs=[pl.BlockSpec(*_scan_rows_spec(nb, lt, tps)),
                  tok, tok, pl.BlockSpec((None, D, D), lambda i: (layer, 0, 0))],
        out_specs=tok,
        out_shape=jax.ShapeDtypeStruct((T, D), F32),
        compiler_params=_cparams("arbitrary"),
        name="rwkv_out",
    )(z, g, x, w)


def _moe_route_kernel(x_ref, g_ref, whi_ref, wlo_ref, b_ref, info_ref, cnt_ref, run_ref):
    h = _rms(x_ref[...], g_ref[...])
    h_hi = h.astype(BF16)
    h_lo = (h - h_hi.astype(F32)).astype(BF16)
    nt = (((1,), (1,)), ((), ()))
    logits = (lax.dot_general(whi_ref[...], h_hi, nt, preferred_element_type=F32)
              + lax.dot_general(whi_ref[...], h_lo, nt, preferred_element_type=F32)
              + lax.dot_general(wlo_ref[...], h_hi, nt, preferred_element_type=F32)
              + b_ref[...])
    gl = [logits[k:k + 1, :] for k in range(N_GROUPS)]
    gmax = functools.reduce(jnp.maximum, gl)
    sel, taken = [], jnp.zeros_like(gmax)
    for k in range(N_GROUPS):
        s = jnp.where((gl[k] == gmax) & (taken == 0.0), 1.0, 0.0)
        taken = taken + s
        sel.append(s)
    p_group = 1.0 / functools.reduce(jnp.add, [jnp.exp(x - gmax) for x in gl])

    el = []
    for m in range(EXP_PER_GROUP):
        rows = [logits[SUBLANES + k * EXP_PER_GROUP + m:SUBLANES + k * EXP_PER_GROUP + m + 1, :]
                for k in range(N_GROUPS)]
        el.append(functools.reduce(jnp.add, [jnp.where(sel[k] > 0.0, rows[k], 0.0) for k in range(N_GROUPS)]))
    emax = functools.reduce(jnp.maximum, el)
    ee = [jnp.exp(x - emax) for x in el]
    esum = functools.reduce(jnp.add, ee)
    prob = [x / esum for x in ee]

    def first_argmax(vals):
        vmax = functools.reduce(jnp.maximum, vals)
        hot, used = [], jnp.zeros_like(vmax)
        for x in vals:
            s = jnp.where((x == vmax) & (used == 0.0), 1.0, 0.0)
            used = used + s
            hot.append(s)
        return hot, vmax

    t1, p1 = first_argmax(prob)
    rest = [jnp.where(t1[m] > 0.0, -1.0, prob[m]) for m in range(EXP_PER_GROUP)]
    t2, p2 = first_argmax(rest)
    scale = p_group / (p1 + p2)
    group = functools.reduce(jnp.add, [sel[k] * float(k * EXP_PER_GROUP) for k in range(N_GROUPS)])
    e1 = group + functools.reduce(jnp.add, [t1[m] * float(m) for m in range(EXP_PER_GROUP)])
    e2 = group + functools.reduce(jnp.add, [t2[m] * float(m) for m in range(EXP_PER_GROUP)])

    @pl.when(pl.program_id(0) == 0)
    def _():
        run_ref[...] = jnp.zeros_like(run_ref)

    tm = e1.shape[1]
    hot1 = jnp.concatenate([sel[k] * t1[m] for k in range(N_GROUPS) for m in range(EXP_PER_GROUP)], axis=0)
    hot2 = jnp.concatenate([sel[k] * t2[m] for k in range(N_GROUPS) for m in range(EXP_PER_GROUP)], axis=0)
    earlier = (lax.broadcasted_iota(jnp.int32, (tm, tm), 0) < lax.broadcasted_iota(jnp.int32, (tm, tm), 1))
    earlier = jnp.where(earlier, 1.0, 0.0).astype(BF16)
    pre1 = _dot(hot1.astype(BF16), earlier)
    pre2 = _dot(hot2.astype(BF16), earlier)
    tot1 = jnp.sum(hot1, axis=1, keepdims=True)
    tot2 = jnp.sum(hot2, axis=1, keepdims=True)
    base = run_ref[:, 0:1]
    rank1 = jnp.sum(hot1 * (base + pre1), axis=0, keepdims=True)
    rank2 = jnp.sum(hot2 * (base + tot1 + pre2), axis=0, keepdims=True)
    run_ref[...] = run_ref[...] + (tot1 + tot2)
    cnt_ref[...] = run_ref[...]

    info_ref[...] = jnp.zeros_like(info_ref)
    info_ref[0:1, :] = e1
    info_ref[1:2, :] = e2
    info_ref[2:3, :] = p1 * scale
    info_ref[3:4, :] = p2 * scale
    info_ref[4:5, :] = rank1
    info_ref[5:6, :] = rank2


def _moe_route(x, g, w_hi, w_lo, b, tm):
    T, D = x.shape
    R = w_hi.shape[0]
    return pl.pallas_call(
        _moe_route_kernel,
        grid=(T // tm,),
        in_specs=[pl.BlockSpec((tm, D), lambda i: (i, 0)),
                  pl.BlockSpec((1, D), lambda i: (0, 0)),
                  pl.BlockSpec((R, D), lambda i: (0, 0)),
                  pl.BlockSpec((R, D), lambda i: (0, 0)),
                  pl.BlockSpec((R, 1), lambda i: (0, 0))],
        out_specs=[pl.BlockSpec((SUBLANES, tm), lambda i: (0, i)),
                   pl.BlockSpec((N_EXPERTS, LANES), lambda i: (0, 0))],
        out_shape=[jax.ShapeDtypeStruct((SUBLANES, T), F32),
                   jax.ShapeDtypeStruct((N_EXPERTS, LANES), F32)],
        scratch_shapes=[pltpu.VMEM((N_EXPERTS, LANES), F32)],
        compiler_params=_cparams("arbitrary"),
        name="moe_route",
    )(x, g, w_hi, w_lo, b)


def _moe_plan(info, counts, tr):
    T = info.shape[1]
    P = 2 * T
    n_tiles = P // tr + N_EXPERTS
    experts = jnp.arange(N_EXPERTS, dtype=jnp.int32)
    eid = info[0:2].astype(jnp.int32).reshape(P)
    rank = info[4:6].astype(jnp.int32).reshape(P)
    tok = jnp.tile(jnp.arange(T, dtype=jnp.int32), 2)
    counts = counts[:, 0].astype(jnp.int32)
    padded = ((counts + tr - 1) // tr) * tr
    ends = jnp.cumsum(padded)
    starts = ends - padded
    slot = jnp.sum(jnp.where(eid[:, None] == experts[None, :], starts[None, :], 0), axis=1) + rank
    src_tok = jnp.zeros((n_tiles * tr,), jnp.int32).at[slot].set(tok)
    n_used = ends[-1] // tr
    tile_row = jnp.minimum(jnp.arange(n_tiles, dtype=jnp.int32), n_used - 1) * tr
    tile_expert = jnp.sum((tile_row[:, None] >= ends[None, :]).astype(jnp.int32), axis=1)
    return src_tok, tile_expert, n_used.reshape(1), slot


def _moe_expert_kernel(tr, src_ref, texp_ref, nused_ref, x_hbm, g_ref, wg_ref, wu_ref, wd_ref, o_ref,
                       xbuf, wg_b, wu_b, wd_b, sem):
    i = pl.program_id(0)
    n_used = nused_ref[0]

    def start_gather(tile, buf):
        def body(r, c):
            tok = src_ref[tile * tr + r]
            pltpu.make_async_copy(x_hbm.at[pl.ds(tok, 1)], xbuf.at[buf, pl.ds(r, 1)], sem.at[buf]).start()
            return c
        lax.fori_loop(0, tr, body, 0, unroll=8)

    def wait_gather(buf):
        pltpu.make_async_copy(x_hbm.at[pl.ds(0, tr)], xbuf.at[buf], sem.at[buf]).wait()

    @pl.when(i == 0)
    def _():
        start_gather(0, 0)

    @pl.when(i + 1 < n_used)
    def _():
        start_gather(i + 1, (i + 1) % 2)

    @pl.when((i == 0) | (texp_ref[i] != texp_ref[jnp.maximum(i - 1, 0)]))
    def _():
        wg_b[...] = wg_ref[0].astype(BF16)
        wu_b[...] = wu_ref[0].astype(BF16)
        wd_b[...] = wd_ref[0].astype(BF16)

    @pl.when(i < n_used)
    def _():
        buf = i % 2
        wait_gather(buf)
        h = _rms(xbuf[buf], g_ref[...]).astype(BF16)
        hg = _dot(h, wg_b[...])
        hu = _dot(h, wu_b[...])
        act = hg * _sigmoid(hg) * hu
        o_ref[...] = _dot(act.astype(BF16), wd_b[...])

    @pl.when(i >= n_used)
    def _():
        o_ref[...] = jnp.zeros_like(o_ref)


def _moe_experts(x, g, src_tok, tile_expert, n_used, w_gate, w_up, w_down, layer, tr):
    T, D = x.shape
    Fd = w_gate.shape[3]
    n_tiles = src_tok.shape[0] // tr
    grid_spec = pltpu.PrefetchScalarGridSpec(
        num_scalar_prefetch=3,
        grid=(n_tiles,),
        in_specs=[pl.BlockSpec(memory_space=pl.ANY),
                  pl.BlockSpec((1, D), lambda i, s, e, n: (0, 0)),
                  pl.BlockSpec((None, 1, D, Fd), lambda i, s, e, n: (layer, e[i], 0, 0)),
                  pl.BlockSpec((None, 1, D, Fd), lambda i, s, e, n: (layer, e[i], 0, 0)),
                  pl.BlockSpec((None, 1, Fd, D), lambda i, s, e, n: (layer, e[i], 0, 0))],
        out_specs=pl.BlockSpec((tr, D), lambda i, s, e, n: (i, 0)),
        scratch_shapes=[pltpu.VMEM((2, tr, D), F32), pltpu.VMEM((D, Fd), BF16), pltpu.VMEM((D, Fd), BF16),
                        pltpu.VMEM((Fd, D), BF16), pltpu.SemaphoreType.DMA((2,))],
    )
    return pl.pallas_call(
        functools.partial(_moe_expert_kernel, tr),
        grid_spec=grid_spec,
        out_shape=jax.ShapeDtypeStruct((n_tiles * tr, D), F32),
        compiler_params=_cparams("arbitrary"),
        name="moe_experts",
    )(src_tok, tile_expert, n_used, x, g, w_gate, w_up, w_down)


def _moe_combine_kernel(tm, slot_ref, info_ref, x_ref, ys_hbm, o_ref, ybuf, sem):
    i = pl.program_id(0)
    n = pl.num_programs(0)
    T = n * tm

    def start_gather(tile, buf):
        def body(r, c):
            for k in range(2):
                s = slot_ref[k * T + tile * tm + r]
                pltpu.make_async_copy(ys_hbm.at[pl.ds(s, 1)], ybuf.at[buf, k, pl.ds(r, 1)], sem.at[buf]).start()
            return c
        lax.fori_loop(0, tm, body, 0, unroll=4)

    def wait_gather(buf):
        for k in range(2):
            pltpu.make_async_copy(ys_hbm.at[pl.ds(0, tm)], ybuf.at[buf, k], sem.at[buf]).wait()

    @pl.when(i == 0)
    def _():
        start_gather(0, 0)

    @pl.when(i + 1 < n)
    def _():
        start_gather(i + 1, (i + 1) % 2)

    buf = i % 2
    wait_gather(buf)
    gates = info_ref[...].T
    o_ref[...] = x_ref[...] + gates[:, 2:3] * ybuf[buf, 0] + gates[:, 3:4] * ybuf[buf, 1]


def _moe_combine(x, info, slot, ys, tm):
    T, D = x.shape
    grid_spec = pltpu.PrefetchScalarGridSpec(
        num_scalar_prefetch=1,
        grid=(T // tm,),
        in_specs=[pl.BlockSpec((SUBLANES, tm), lambda i, s: (0, i)),
                  pl.BlockSpec((tm, D), lambda i, s: (i, 0)),
                  pl.BlockSpec(memory_space=pl.ANY)],
        out_specs=pl.BlockSpec((tm, D), lambda i, s: (i, 0)),
        scratch_shapes=[pltpu.VMEM((2, 2, tm, D), F32), pltpu.SemaphoreType.DMA((2,))],
    )
    return pl.pallas_call(
        functools.partial(_moe_combine_kernel, tm),
        grid_spec=grid_spec,
        out_shape=jax.ShapeDtypeStruct((T, D), F32),
        compiler_params=_cparams("arbitrary"),
        name="moe_combine",
    )(slot, info, x, ys)


def _ple_kernel(x_ref, xr_ref, p_ref, g_ref, wg_ref, wp_ref, o_ref, h_ref):
    @pl.when(pl.program_id(1) == 0)
    def _():
        h_ref[...] = _rms(x_ref[...], g_ref[...]).astype(BF16)

    gate = _sigmoid(_dot(h_ref[...], wg_ref[...]))
    o_ref[...] = xr_ref[...] + gate * _dot(p_ref[...].astype(BF16), wp_ref[...])


def _ple(x, p, g, w_gate, w_proj, layer, tm, tn):
    T, D = x.shape
    P = p.shape[2]
    return pl.pallas_call(
        _ple_kernel,
        grid=(T // tm, D // tn),
        in_specs=[pl.BlockSpec((tm, D), lambda i, n: (i, 0)),
                  pl.BlockSpec((tm, tn), lambda i, n: (i, n)),
                  pl.BlockSpec((None, tm, P), lambda i, n: (layer, i, 0)),
                  pl.BlockSpec((1, D), lambda i, n: (0, 0)),
                  pl.BlockSpec((None, D, tn), lambda i, n: (layer, 0, n)),
                  pl.BlockSpec((None, P, tn), lambda i, n: (layer, 0, n))],
        out_specs=pl.BlockSpec((tm, tn), lambda i, n: (i, n)),
        out_shape=jax.ShapeDtypeStruct((T, D), F32),
        scratch_shapes=[pltpu.VMEM((tm, D), BF16)],
        compiler_params=_cparams("arbitrary", "arbitrary"),
        name="ple",
    )(x, x, p, g, w_gate, w_proj)


def _final_norm_kernel(x_ref, g_ref, o_ref):
    o_ref[...] = _rms(x_ref[...], g_ref[...])


def _final_norm(x, g, tm):
    T, D = x.shape
    return pl.pallas_call(
        _final_norm_kernel,
        grid=(T // tm,),
        in_specs=[pl.BlockSpec((tm, D), lambda i: (i, 0)), pl.BlockSpec((1, D), lambda i: (0, 0))],
        out_specs=pl.BlockSpec((tm, D), lambda i: (i, 0)),
        out_shape=jax.ShapeDtypeStruct((T, D), F32),
        compiler_params=_cparams("arbitrary"),
        name="final_norm",
    )(x, g)


def _chain_tile(p, H):
    t = p.reshape(H // 2, 2, RW_HEAD_DIM).transpose(2, 1, 0)
    bg = LANES // H
    return jnp.broadcast_to(t[:, :, None, :], (RW_HEAD_DIM, 2, bg, H // 2)).reshape(RW_HEAD_DIM, LANES)


def _trunk(x, p, conv0, shift0, wkv0, W, prompt):
    B, L, D = x.shape
    depth = p.shape[0]
    T = B * L
    d_a = W["vn_g"].shape[1]
    H = D // RW_HEAD_DIM
    lc = min(L, CHUNK)
    x = x.reshape(T, D)
    p = p.reshape(depth, T, -1)
    tm = 512
    tm_mix = 256
    tr = 256 if T >= 4096 else 128
    chunk_v, conv_new, shift_new, wkv_stack = [], [], [], None
    for i in range(depth):
        j = i // 2
        if i % 2 == 0:
            proj = _even_inproj(x, W["g_mix"][i][None], W["w_in"], j, d_a, tm, d_a)
            ws_eff = jnp.tile(W["w_s"][j][:, :lc, :lc], (1, CHUNK // lc, CHUNK // lc))
            bias = jnp.tile(jnp.repeat(W["b_s"][j][:, :lc].T, A_GROUP_DIM, axis=1), (CHUNK // lc, 1))
            st = conv0[j]
            if prompt:
                b1 = b2 = st
            else:
                zero = jnp.zeros((B, L, st.shape[-1]), F32)
                b1 = zero.at[:, 0].set(st[:, 1]).reshape(T, -1)
                b2 = zero.at[:, 0].set(st[:, 0]).at[:, 1].set(st[:, 1]).reshape(T, -1)
            x, v_all, cx_all = _even_mix(proj, x, W["vn_g"][j][None], W["vn_b"][j][None], ws_eff, bias,
                                         W["conv_w"][j], W["w_out"], j, b1, b2, prompt, L, tm_mix, d_a)
            start = ((L - 1) // CHUNK) * CHUNK
            chunk_v.append(v_all.reshape(B, L, -1)[:, start:])
            conv_new.append(cx_all.reshape(B, L, -1)[:, L - 2:])
        else:
            st = shift0[j]
            if prompt:
                bnd = st[:, None, :]
            else:
                bnd = jnp.zeros((B, L, D), F32).at[:, 0].set(st).reshape(T, D)
            xrkv, xw, xa, xg, h_all = _rwkv_prep(x, W["g_mix"][i][None], W["rw_mu"][j], bnd, prompt, L, tm_mix)
            rkv = _bmm(xrkv, W["rw_w_rkv"], j, tm, B, L)
            decay = _lora(xw, W["rw_w1"][j], W["rw_w2"][j], W["rw_w0"][j][None], "tanh", "decay", tm, (B, L))
            a = _lora(xa, W["rw_a1"][j], W["rw_a2"][j], W["rw_a0"][j][None], "none", "sigmoid", tm, (B, L))
            g = _lora(xg, W["rw_g1"][j], W["rw_g2"][j], jnp.zeros((1, D), F32), "sigmoid", "none", tm)
            consts = (_chain_tile(W["rw_k_k"][j], H), _chain_tile(W["rw_k_a"][j], H),
                      _chain_tile(W["rw_r_k"][j].reshape(-1), H),
                      _chain_tile(W["rw_gn_g"][j], H), _chain_tile(W["rw_gn_b"][j], H))
            z, wkv_stack = _wkv_scan(rkv, decay, a, wkv0, j, consts, min(L, 32), wkv_stack)
            x = _rwkv_out(z, g, x, W["rw_w_o"], j, tm_mix, L)
            shift_new.append(h_all.reshape(B, L, D)[:, -1])
        info, counts = _moe_route(x, W["g_ffn"][i][None], W["moe_r_hi"][i], W["moe_r_lo"][i], W["moe_r_b"][i], tm)
        src_tok, tile_expert, n_used, slot = _moe_plan(info, counts, tr)
        ys = _moe_experts(x, W["g_ffn"][i][None], src_tok, tile_expert, n_used,
                          W["moe_w_gate"], W["moe_w_up"], W["moe_w_down"], i, tr)
        x = _moe_combine(x, info, slot, ys, tm_mix)
        x = _ple(x, p, W["ple_g"][i][None], W["ple_w_gate"], W["ple_w_proj"], i, tm, 1024)
    y = _final_norm(x, W["g_final"][None], tm).reshape(B, L, D)
    return y, jnp.stack(chunk_v), jnp.stack(conv_new), jnp.stack(shift_new), wkv_stack


def _pad_rank(w1, w2):
    r = w1.shape[-1]
    rp = -(-r // LANES) * LANES
    w1 = jnp.pad(w1, ((0, 0), (0, 0), (0, rp - r)))
    w2 = jnp.pad(w2, ((0, 0), (0, rp - r), (0, 0)))
    return w1.astype(BF16), w2.astype(BF16)


def kernel(x_prompt, x_sample, state_conv, state_shift, state_wkv, p_prompt, p_sample, g_mix, g_ffn, g_final, w_in_even, vn_g, vn_b, w_s, b_s, conv_w, w_out_even, rw_mu, rw_w_rkv, rw_w0, rw_w1, rw_w2, rw_a0, rw_a1, rw_a2, rw_g1, rw_g2, rw_k_k, rw_k_a, rw_r_k, rw_gn_g, rw_gn_b, rw_w_o, moe_w_gr, moe_b_gr, moe_w_er, moe_b_er, moe_w_gate, moe_w_up, moe_w_down, ple_g, ple_w_gate, ple_w_proj):
    depth, D = g_mix.shape
    r_w = jnp.zeros((depth, 32, D), F32)
    r_w = r_w.at[:, :N_GROUPS].set(jnp.swapaxes(moe_w_gr, 1, 2))
    r_w = r_w.at[:, SUBLANES:SUBLANES + N_EXPERTS].set(jnp.swapaxes(moe_w_er, 1, 2))
    r_hi = r_w.astype(BF16)
    r_lo = (r_w - r_hi.astype(F32)).astype(BF16)
    r_b = jnp.zeros((depth, 32, 1), F32)
    r_b = r_b.at[:, :N_GROUPS, 0].set(moe_b_gr).at[:, SUBLANES:SUBLANES + N_EXPERTS, 0].set(moe_b_er)
    w1, w2 = _pad_rank(rw_w1, rw_w2)
    a1, a2 = _pad_rank(rw_a1, rw_a2)
    g1, g2 = _pad_rank(rw_g1, rw_g2)
    W = dict(g_mix=g_mix, g_ffn=g_ffn, g_final=g_final,
             w_in=w_in_even.astype(BF16), vn_g=vn_g, vn_b=vn_b, w_s=w_s, b_s=b_s, conv_w=conv_w,
             w_out=w_out_even.astype(BF16),
             rw_mu=rw_mu, rw_w_rkv=rw_w_rkv.astype(BF16), rw_w0=rw_w0, rw_w1=w1, rw_w2=w2,
             rw_a0=rw_a0, rw_a1=a1, rw_a2=a2, rw_g1=g1, rw_g2=g2,
             rw_k_k=rw_k_k, rw_k_a=rw_k_a, rw_r_k=rw_r_k, rw_gn_g=rw_gn_g, rw_gn_b=rw_gn_b,
             rw_w_o=rw_w_o.astype(BF16),
             moe_r_hi=r_hi, moe_r_lo=r_lo, moe_r_b=r_b,
             moe_w_gate=moe_w_gate, moe_w_up=moe_w_up, moe_w_down=moe_w_down,
             ple_g=ple_g, ple_w_gate=ple_w_gate.astype(BF16), ple_w_proj=ple_w_proj.astype(BF16))
    bp = x_prompt.shape[0]
    n_even, n_odd = state_conv.shape[0], state_shift.shape[0]
    conv0 = jnp.zeros((n_even, bp) + state_conv.shape[2:], F32)
    shift0 = jnp.zeros((n_odd, bp, D), F32)
    wkv0 = jnp.zeros((n_odd, bp) + state_wkv.shape[2:], F32)
    y_p, cv_p, conv_p, shift_p, wkv_p = _trunk(x_prompt, p_prompt, conv0, shift0, wkv0, W, True)
    y_s, cv_s, conv_s, shift_s, wkv_s = _trunk(x_sample, p_sample, state_conv, state_shift, state_wkv, W, False)
    return (y_p, y_s, cv_p, conv_p, shift_p, wkv_p, cv_s, conv_s, shift_s, wkv_s)
```

```python
import functools

import jax
import jax.numpy as jnp
from jax import lax
from jax.experimental import pallas as pl
from jax.experimental.pallas import tpu as pltpu

F32 = jnp.float32
BF16 = jnp.bfloat16

NORM_EPS = 1e-6
LN_EPS = 1e-5
RW_GN_EPS = 64e-5
CHUNK = 128
A_GROUP_DIM = 128
RW_HEAD_DIM = 64
N_GROUPS = 4
EXP_PER_GROUP = 4
N_EXPERTS = N_GROUPS * EXP_PER_GROUP

LANES = 128
SUBLANES = 8
VMEM_LIMIT = 52 * 1024 * 1024


def _cparams(*sem):
    return pltpu.CompilerParams(dimension_semantics=sem, vmem_limit_bytes=VMEM_LIMIT)


def _rms(x, g):
    return x * lax.rsqrt(jnp.mean(x * x, axis=-1, keepdims=True) + NORM_EPS) * g


def _gelu_tanh(x):
    return 0.5 * x * (1.0 + jnp.tanh(0.7978845608028654 * (x + 0.044715 * (x * x * x))))


def _sigmoid(x):
    return 1.0 / (1.0 + jnp.exp(-x))


def _dot(a, b):
    return jnp.dot(a, b, preferred_element_type=F32)


ROW_CHUNK = 256


def _row_chunks(tm):
    ch = min(tm, ROW_CHUNK)
    return [slice(c * ch, (c + 1) * ch) for c in range(tm // ch)]


def _even_inproj_kernel(n_gelu, x_ref, g_ref, w_ref, o_ref, h_ref):
    n = pl.program_id(1)

    def run(first, gelu):
        for rows in _row_chunks(x_ref.shape[0]):
            if first:
                h = _rms(x_ref[rows, :], g_ref[...]).astype(BF16)
                h_ref[rows, :] = h
            else:
                h = h_ref[rows, :]
            acc = _dot(h, w_ref[...])
            o_ref[rows, :] = _gelu_tanh(acc) if gelu else acc

    pl.when(n == 0)(lambda: run(True, n_gelu > 0))
    pl.when((n > 0) & (n < n_gelu))(lambda: run(False, True))
    pl.when((n > 0) & (n >= n_gelu))(lambda: run(False, False))


def _even_inproj(x, g, w, layer, d_a, tm, tn):
    T, D = x.shape
    N = w.shape[2]
    return pl.pallas_call(
        functools.partial(_even_inproj_kernel, (2 * d_a) // tn),
        grid=(T // tm, N // tn),
        in_specs=[pl.BlockSpec((tm, D), lambda i, n: (i, 0)),
                  pl.BlockSpec((1, D), lambda i, n: (0, 0)),
                  pl.BlockSpec((None, D, tn), lambda i, n: (layer, 0, n))],
        out_specs=pl.BlockSpec((tm, tn), lambda i, n: (i, n)),
        out_shape=jax.ShapeDtypeStruct((T, N), F32),
        scratch_shapes=[pltpu.VMEM((tm, D), BF16)],
        compiler_params=_cparams("arbitrary", "arbitrary"),
        name="even_inproj",
    )(x, g, w)


def _even_mix_kernel(prompt, seq_len, tiles_per_seq, tm, d_a,
                     u_ref, v_ref, xi_ref, gb_ref, gc_ref, x_ref, vng_ref, vnb_ref,
                     ws_ref, bias_ref, cw_ref, wout_ref, b1_ref, b2_ref,
                     xo_ref, vo_ref, cxo_ref, y_ref, carry_ref):
    i = pl.program_id(0)
    lc = min(seq_len, CHUNK)

    vg = v_ref[...]
    mu = jnp.mean(vg, axis=-1, keepdims=True)
    vc = vg - mu
    var = jnp.mean(vc * vc, axis=-1, keepdims=True)
    vln = vc * lax.rsqrt(var + LN_EPS) * vng_ref[...] + vnb_ref[...]
    vo_ref[...] = vln

    r = lax.broadcasted_iota(jnp.int32, (CHUNK, CHUNK), 0)
    c = lax.broadcasted_iota(jnp.int32, (CHUNK, CHUNK), 1)
    keep = (r >= c) & ((r // lc) == (c // lc))
    n_groups = d_a // A_GROUP_DIM
    for g in range(n_groups):
        wg = jnp.where(keep, ws_ref[g], 0.0).astype(BF16)
        cols = slice(g * A_GROUP_DIM, (g + 1) * A_GROUP_DIM)
        for ch in range(tm // CHUNK):
            rows = slice(ch * CHUNK, (ch + 1) * CHUNK)
            mixed = _dot(wg, vln[rows, cols].astype(BF16)) + bias_ref[:, cols]
            y_ref[rows, cols] = (u_ref[rows, cols] * mixed).astype(BF16)

    cx = gc_ref[...] * xi_ref[...]
    cxo_ref[...] = cx
    row = lax.broadcasted_iota(jnp.int32, cx.shape, 0)
    r1 = pltpu.roll(cx, 1, 0)
    r2 = pltpu.roll(cx, 2, 0)
    if prompt:
        @pl.when(i % tiles_per_seq == 0)
        def _():
            carry_ref[...] = jnp.zeros_like(carry_ref)
            carry_ref[SUBLANES - 2:SUBLANES, :] = b1_ref[0]
        p1 = carry_ref[SUBLANES - 1:SUBLANES, :]
        p2 = carry_ref[SUBLANES - 2:SUBLANES - 1, :]
        s1 = jnp.where(row == 0, p1, r1)
        s2 = jnp.where(row == 0, p2, jnp.where(row == 1, p1, r2))
        carry_ref[...] = cx[tm - SUBLANES:tm, :]
    else:
        pos = row % seq_len
        s1 = jnp.where(pos == 0, b1_ref[...], r1)
        s2 = jnp.where(pos <= 1, b2_ref[...], r2)
    conv = s2 * cw_ref[0:1, :] + s1 * cw_ref[1:2, :] + cx * cw_ref[2:3, :]
    y_ref[:, d_a:] = (gb_ref[...] * conv).astype(BF16)

    xo_ref[...] = x_ref[...] + _dot(y_ref[...], wout_ref[...])


def _even_mix(proj, x, vn_g, vn_b, ws_eff, bias, conv_w, w_out, layer, b1, b2, prompt, seq_len, tm, d_a):
    T, D = x.shape
    d_b = D - d_a
    tps = max(seq_len // tm, 1)
    col = lambda k: pl.BlockSpec((tm, d_a), lambda i, k=k: (i, k))
    const = lambda shape: pl.BlockSpec(shape, lambda i: (0,) * len(shape))
    if prompt:
        b1_spec = pl.BlockSpec((1, 2, d_b), lambda i: (i // tps, 0, 0))
        b2_spec = pl.BlockSpec((1, 2, d_b), lambda i: (i // tps, 0, 0))
    else:
        b1_spec = pl.BlockSpec((tm, d_b), lambda i: (i, 0))
        b2_spec = pl.BlockSpec((tm, d_b), lambda i: (i, 0))
    return pl.pallas_call(
        functools.partial(_even_mix_kernel, prompt, seq_len, tps, tm, d_a),
        grid=(T // tm,),
        in_specs=[col(0), col(1), col(2), col(3), col(4),
                  pl.BlockSpec((tm, D), lambda i: (i, 0)),
                  const((1, d_a)), const((1, d_a)),
                  const(ws_eff.shape), const(bias.shape), const(conv_w.shape),
                  pl.BlockSpec((None,) + w_out.shape[1:], lambda i: (layer, 0, 0)),
                  b1_spec, b2_spec],
        out_specs=[pl.BlockSpec((tm, D), lambda i: (i, 0)),
                   pl.BlockSpec((tm, d_a), lambda i: (i, 0)),
                   pl.BlockSpec((tm, d_b), lambda i: (i, 0))],
        out_shape=[jax.ShapeDtypeStruct((T, D), F32),
                   jax.ShapeDtypeStruct((T, d_a), F32),
                   jax.ShapeDtypeStruct((T, d_b), F32)],
        scratch_shapes=[pltpu.VMEM((tm, D), BF16), pltpu.VMEM((SUBLANES, d_b), F32)],
        compiler_params=_cparams("arbitrary"),
        name="even_mix",
    )(proj, proj, proj, proj, proj, x, vn_g, vn_b, ws_eff, bias, conv_w, w_out, b1, b2)


def _rwkv_prep_kernel(prompt, seq_len, tiles_per_seq, tm,
                      x_ref, g_ref, mu_ref, b_ref,
                      rkv_ref, xw_ref, xa_ref, xg_ref, h_ref, carry_ref):
    i = pl.program_id(0)
    h = _rms(x_ref[...], g_ref[...])
    h_ref[...] = h
    row = lax.broadcasted_iota(jnp.int32, h.shape, 0)
    r1 = pltpu.roll(h, 1, 0)
    if prompt:
        @pl.when(i % tiles_per_seq == 0)
        def _():
            carry_ref[...] = jnp.zeros_like(carry_ref)
            carry_ref[SUBLANES - 1:SUBLANES, :] = b_ref[0]
        prev = jnp.where(row == 0, carry_ref[SUBLANES - 1:SUBLANES, :], r1)
        carry_ref[...] = h[tm - SUBLANES:tm, :]
    else:
        prev = jnp.where(row % seq_len == 0, b_ref[...], r1)
    xx = prev - h
    rkv_ref[0] = (h + xx * mu_ref[0:1, :]).astype(BF16)
    xw_ref[...] = (h + xx * mu_ref[1:2, :]).astype(BF16)
    rkv_ref[1] = (h + xx * mu_ref[2:3, :]).astype(BF16)
    rkv_ref[2] = (h + xx * mu_ref[3:4, :]).astype(BF16)
    xa_ref[...] = (h + xx * mu_ref[4:5, :]).astype(BF16)
    xg_ref[...] = (h + xx * mu_ref[5:6, :]).astype(BF16)


def _rwkv_prep(x, g, mu, bnd, prompt, seq_len, tm):
    T, D = x.shape
    tps = max(seq_len // tm, 1)
    if prompt:
        b_spec = pl.BlockSpec((1, 1, D), lambda i: (i // tps, 0, 0))
    else:
        b_spec = pl.BlockSpec((tm, D), lambda i: (i, 0))
    tok = pl.BlockSpec((tm, D), lambda i: (i, 0))
    return pl.pallas_call(
        functools.partial(_rwkv_prep_kernel, prompt, seq_len, tps, tm),
        grid=(T // tm,),
        in_specs=[tok, pl.BlockSpec((1, D), lambda i: (0, 0)),
                  pl.BlockSpec(mu.shape, lambda i: (0, 0)), b_spec],
        out_specs=[pl.BlockSpec((3, tm, D), lambda i: (0, i, 0)), tok, tok, tok, tok],
        out_shape=[jax.ShapeDtypeStruct((3, T, D), BF16),
                   jax.ShapeDtypeStruct((T, D), BF16),
                   jax.ShapeDtypeStruct((T, D), BF16),
                   jax.ShapeDtypeStruct((T, D), BF16),
                   jax.ShapeDtypeStruct((T, D), F32)],
        scratch_shapes=[pltpu.VMEM((SUBLANES, D), F32)],
        compiler_params=_cparams("arbitrary"),
        name="rwkv_prep",
    )(x, g, mu, bnd)


HEAD_PAIRS = 16
GROUP_BATCH = 4


def _store_scan_rows(o_ref, lead, acc, nb, lt, row0=0):
    rows = acc.shape[0]
    if nb == 1:
        o_ref[lead + (0, slice(row0, row0 + rows))] = acc.reshape(rows, HEAD_PAIRS, LANES)
    else:
        per_group = GROUP_BATCH * lt
        g0, ng = row0 // per_group, rows // per_group
        a5 = acc.reshape(ng, GROUP_BATCH, lt, HEAD_PAIRS, LANES)
        for bl in range(GROUP_BATCH):
            o_ref[lead + (slice(g0, g0 + ng), slice(None), slice(bl * HEAD_PAIRS, (bl + 1) * HEAD_PAIRS))] = a5[:, bl]


def _load_scan_rows(z_ref, nb, lt):
    if nb == 1:
        return z_ref[0].reshape(lt, HEAD_PAIRS * LANES)
    parts = [z_ref[:, :, bl * HEAD_PAIRS:(bl + 1) * HEAD_PAIRS, :] for bl in range(GROUP_BATCH)]
    return jnp.stack(parts, axis=1).reshape(nb * lt, HEAD_PAIRS * LANES)


def _scan_rows_spec(nb, lt, tps, lead=()):
    nl = len(lead)
    if nb == 1:
        shape = (1,) * nl + (1, lt, HEAD_PAIRS, LANES)
        return shape, lambda i: (i // tps // GROUP_BATCH, i % tps, (i // tps) % GROUP_BATCH, 0)
    shape = (1,) * nl + (nb // GROUP_BATCH, lt, GROUP_BATCH * HEAD_PAIRS, LANES)
    return shape, lambda i: (i, 0, 0, 0)


def _tile_geometry(tm, seq_len):
    lt = min(tm, seq_len)
    return tm // lt, lt, seq_len // lt


def _bmm_kernel(nb, lt, x_ref, w_ref, o_ref):
    _store_scan_rows(o_ref, (0,), _dot(x_ref[0], w_ref[0]), nb, lt)


def _bmm(x, w, layer, tm, B, L):
    n3, T, D = x.shape
    N = w.shape[3]
    nb, lt, tps = _tile_geometry(tm, L)
    shape, idx = _scan_rows_spec(nb, lt, tps, lead=(0,))
    return pl.pallas_call(
        functools.partial(_bmm_kernel, nb, lt),
        grid=(n3, T // tm),
        in_specs=[pl.BlockSpec((1, tm, D), lambda b, i: (b, i, 0)),
                  pl.BlockSpec((None, 1, D, N), lambda b, i: (layer, b, 0, 0))],
        out_specs=pl.BlockSpec(shape, lambda b, i: (b,) + idx(i)),
        out_shape=jax.ShapeDtypeStruct((n3, B // GROUP_BATCH, L, GROUP_BATCH * HEAD_PAIRS, LANES), F32),
        compiler_params=_cparams("arbitrary", "arbitrary"),
        name="rwkv_rkv_proj",
    )(x, w)


def _lora_kernel(mid, out, geom, x_ref, w1_ref, w2_ref, b_ref, o_ref):
    for rows in _row_chunks(x_ref.shape[0]):
        t = _dot(x_ref[rows, :], w1_ref[...])
        if mid == "tanh":
            t = jnp.tanh(t)
        elif mid == "sigmoid":
            t = _sigmoid(t)
        z = _dot(t.astype(BF16), w2_ref[...]) + b_ref[...]
        if out == "decay":
            w_log = jnp.minimum(z, 0.0) - jnp.log(1.0 + jnp.exp(-jnp.abs(z))) - 0.5
            z = jnp.exp(-jnp.exp(w_log))
        elif out == "sigmoid":
            z = _sigmoid(z)
        if geom is None:
            o_ref[rows, :] = z
        else:
            _store_scan_rows(o_ref, (), z, *geom, row0=rows.start)


def _lora(x, w1, w2, b, mid, out, tm, head_pairs=None):
    T, D = x.shape
    R = w1.shape[1]
    if head_pairs is None:
        geom = None
        out_spec = pl.BlockSpec((tm, D), lambda i: (i, 0))
        out_shape = jax.ShapeDtypeStruct((T, D), F32)
    else:
        B, L = head_pairs
        nb, lt, tps = _tile_geometry(tm, L)
        geom = (nb, lt)
        out_spec = pl.BlockSpec(*_scan_rows_spec(nb, lt, tps))
        out_shape = jax.ShapeDtypeStruct((B // GROUP_BATCH, L, GROUP_BATCH * HEAD_PAIRS, LANES), F32)
    return pl.pallas_call(
        functools.partial(_lora_kernel, mid, out, geom),
        grid=(T // tm,),
        in_specs=[pl.BlockSpec((tm, D), lambda i: (i, 0)),
                  pl.BlockSpec((D, R), lambda i: (0, 0)),
                  pl.BlockSpec((R, D), lambda i: (0, 0)),
                  pl.BlockSpec((1, D), lambda i: (0, 0))],
        out_specs=out_spec,
        out_shape=out_shape,
        compiler_params=_cparams("arbitrary"),
        name="rwkv_lora_" + out,
    )(x, w1, w2, b)


def _wkv_scan_kernel(tb, aliased, r_ref, k_ref, v_ref, w_ref, a_ref, s0_ref, kk_c, ka_c, rk_c, gg_c, gb_c, *rest):
    z_ref, st_ref, s_ref, tile_ref, vec_ref, zbuf_ref, slab_ref = rest[1:] if aliased else rest
    n = RW_HEAD_DIM
    half = LANES // 2
    step0 = pl.program_id(1)
    seq_refs = (r_ref, w_ref, k_ref, v_ref, a_ref)
    R, W, K, V, A = range(5)

    nchunk = n * n // LANES
    per_parity = LANES // 2 // GROUP_BATCH
    chain_rows = [pl.ds(b * 2 * per_parity + hl, per_parity, stride=2) for hl in range(2) for b in range(GROUP_BATCH)]

    @pl.when(step0 == 0)
    def _():
        zbuf_ref[...] = jnp.zeros_like(zbuf_ref)
        x2 = s0_ref[...].reshape(LANES, n * n)
        for c in range(nchunk):
            slab_ref[c] = x2[:, c * LANES:(c + 1) * LANES]
        for c in range(nchunk):
            xt = jnp.concatenate([slab_ref[c, rows, :] for rows in chain_rows], axis=0).T
            for il in range(LANES // n):
                s_ref[pl.ds(c * (LANES // n) + il, n, stride=n), :] = xt[il * n:(il + 1) * n]

    def fetch(t, slot):
        for q, ref in enumerate(seq_refs):
            xt = ref[(0,) * (len(ref.shape) - 3) + (t,)].T
            tile_ref[slot, q] = jnp.concatenate([xt[:n], xt[n:]], axis=1)

    def flush(t, slot):
        z = zbuf_ref[slot]
        z_ref[0, t] = jnp.concatenate([z[:, :half], z[:, half:]], axis=0).T

    def recur(slot):
        k_t = tile_ref[slot, K]
        a_t = tile_ref[slot, A]
        v_t = tile_ref[slot, V]
        kk = k_t * kk_c[...]
        nrm = jnp.sqrt(jnp.sum(kk * kk, axis=0, keepdims=True))
        kk = kk / jnp.maximum(nrm, 1e-12)
        kmod = k_t * (1.0 + (a_t - 1.0) * ka_c[...])
        vec_ref[slot, 0] = kk
        vec_ref[slot, 1] = kk * a_t
        vec_ref[slot, 2] = kmod

        sa = jnp.zeros((n, LANES), F32)
        for j in range(n):
            sa = sa + s_ref[j * n:(j + 1) * n, :] * vec_ref[slot, 0, j:j + 1, :]
        y = jnp.zeros((n, LANES), F32)
        for j in range(n):
            s_new = (s_ref[j * n:(j + 1) * n, :] * tile_ref[slot, W, j:j + 1, :]
                     - sa * vec_ref[slot, 1, j:j + 1, :] + v_t * vec_ref[slot, 2, j:j + 1, :])
            s_ref[j * n:(j + 1) * n, :] = s_new
            y = y + s_new * tile_ref[slot, R, j:j + 1, :]

        mu = jnp.mean(y, axis=0, keepdims=True)
        yc = y - mu
        var = jnp.mean(yc * yc, axis=0, keepdims=True)
        yn = yc * lax.rsqrt(var + RW_GN_EPS) * gg_c[...] + gb_c[...]
        bonus = jnp.sum(tile_ref[slot, R] * kmod * rk_c[...], axis=0, keepdims=True) * v_t
        zbuf_ref[slot] = yn + bonus

    fetch(0, 0)

    def pair(u, carry):
        t = 2 * u
        fetch(t + 1, 1)
        flush(jnp.maximum(t - 1, 0), 1)
        recur(0)
        fetch(jnp.minimum(t + 2, tb - 1), 0)
        flush(t, 0)
        recur(1)
        return carry

    lax.fori_loop(0, tb // 2, pair, 0)
    flush(tb - 1, 1)

    @pl.when(step0 == pl.num_programs(1) - 1)
    def _():
        for c in range(nchunk):
            xt = jnp.concatenate([s_ref[pl.ds(c * (LANES // n) + il, n, stride=n), :]
                                  for il in range(LANES // n)], axis=0).T
            for q, rows in enumerate(chain_rows):
                slab_ref[c, rows, :] = xt[q * per_parity:(q + 1) * per_parity]
        x2 = jnp.concatenate([slab_ref[c] for c in range(nchunk)], axis=1)
        st_ref[...] = x2.reshape(st_ref.shape)


def _wkv_scan(rkv, w, a, state, layer, consts, tb, stacked=None):
    _, G, L, half, _ = rkv.shape
    n = RW_HEAD_DIM
    H = state.shape[2]
    assert tb % 2 == 0 and L % tb == 0 and half == LANES // 2 and GROUP_BATCH * H == LANES
    seq3 = lambda m: pl.BlockSpec((1, 1, tb, half, LANES), lambda g, t, m=m: (m, g, t, 0, 0))
    seq = pl.BlockSpec((1, tb, half, LANES), lambda g, t: (g, t, 0, 0))
    st = pl.BlockSpec((None, GROUP_BATCH, H, n, n), lambda g, t: (layer, g, 0, 0, 0))
    cst = pl.BlockSpec((n, LANES), lambda g, t: (0, 0))
    extra, extra_specs, aliases = (), [], {}
    if stacked is not None:
        extra, extra_specs, aliases = (stacked,), [pl.BlockSpec(memory_space=pl.ANY)], {11: 1}
    return pl.pallas_call(
        functools.partial(_wkv_scan_kernel, tb, stacked is not None),
        grid=(G, L // tb),
        in_specs=[seq3(0), seq3(1), seq3(2), seq, seq, st, cst, cst, cst, cst, cst] + extra_specs,
        out_specs=[seq, st],
        out_shape=[jax.ShapeDtypeStruct((G, L, half, LANES), F32),
                   jax.ShapeDtypeStruct(state.shape, F32)],
        scratch_shapes=[pltpu.VMEM((n * n, LANES), F32), pltpu.VMEM((2, 5, n, LANES), F32),
                        pltpu.VMEM((2, 3, n, LANES), F32), pltpu.VMEM((2, n, LANES), F32),
                        pltpu.VMEM((n * n // LANES, LANES, LANES), F32)],
        input_output_aliases=aliases,
        compiler_params=_cparams("arbitrary", "arbitrary"),
        name="rwkv_scan",
    )(rkv, rkv, rkv, w, a, state, *consts, *extra)


def _rwkv_out_kernel(nb, lt, z_ref, g_ref, x_ref, w_ref, o_ref):
    z = _load_scan_rows(z_ref, nb, lt)
    o_ref[...] = x_ref[...] + _dot((z * g_ref[...]).astype(BF16), w_ref[...])


def _rwkv_out(z, g, x, w, layer, tm, L):
    T, D = x.shape
    nb, lt, tps = _tile_geometry(tm, L)
    tok = pl.BlockSpec((tm, D), lambda i: (i, 0))
    return pl.pallas_call(
        functools.partial(_rwkv_out_kernel, nb, lt),
        grid=(T // tm,),
        in_specs=[pl.BlockSpec(*_scan_rows_spec(nb, lt, tps)),
                  tok, tok, pl.BlockSpec((None, D, D), lambda i: (layer, 0, 0))],
        out_specs=tok,
        out_shape=jax.ShapeDtypeStruct((T, D), F32),
        compiler_params=_cparams("arbitrary"),
        name="rwkv_out",
    )(z, g, x, w)


def _moe_route_kernel(x_ref, g_ref, whi_ref, wlo_ref, b_ref, info_ref, cnt_ref, run_ref):
    h = _rms(x_ref[...], g_ref[...])
    h_hi = h.astype(BF16)
    h_lo = (h - h_hi.astype(F32)).astype(BF16)
    nt = (((1,), (1,)), ((), ()))
    logits = (lax.dot_general(whi_ref[...], h_hi, nt, preferred_element_type=F32)
              + lax.dot_general(whi_ref[...], h_lo, nt, preferred_element_type=F32)
              + lax.dot_general(wlo_ref[...], h_hi, nt, preferred_element_type=F32)
              + b_ref[...])
    gl = [logits[k:k + 1, :] for k in range(N_GROUPS)]
    gmax = functools.reduce(jnp.maximum, gl)
    sel, taken = [], jnp.zeros_like(gmax)
    for k in range(N_GROUPS):
        s = jnp.where((gl[k] == gmax) & (taken == 0.0), 1.0, 0.0)
        taken = taken + s
        sel.append(s)
    p_group = 1.0 / functools.reduce(jnp.add, [jnp.exp(x - gmax) for x in gl])

    el = []
    for m in range(EXP_PER_GROUP):
        rows = [logits[SUBLANES + k * EXP_PER_GROUP + m:SUBLANES + k * EXP_PER_GROUP + m + 1, :]
                for k in range(N_GROUPS)]
        el.append(functools.reduce(jnp.add, [jnp.where(sel[k] > 0.0, rows[k], 0.0) for k in range(N_GROUPS)]))
    emax = functools.reduce(jnp.maximum, el)
    ee = [jnp.exp(x - emax) for x in el]
    esum = functools.reduce(jnp.add, ee)
    prob = [x / esum for x in ee]

    def first_argmax(vals):
        vmax = functools.reduce(jnp.maximum, vals)
        hot, used = [], jnp.zeros_like(vmax)
        for x in vals:
            s = jnp.where((x == vmax) & (used == 0.0), 1.0, 0.0)
            used = used + s
            hot.append(s)
        return hot, vmax

    t1, p1 = first_argmax(prob)
    rest = [jnp.where(t1[m] > 0.0, -1.0, prob[m]) for m in range(EXP_PER_GROUP)]
    t2, p2 = first_argmax(rest)
    scale = p_group / (p1 + p2)
    group = functools.reduce(jnp.add, [sel[k] * float(k * EXP_PER_GROUP) for k in range(N_GROUPS)])
    e1 = group + functools.reduce(jnp.add, [t1[m] * float(m) for m in range(EXP_PER_GROUP)])
    e2 = group + functools.reduce(jnp.add, [t2[m] * float(m) for m in range(EXP_PER_GROUP)])

    @pl.when(pl.program_id(0) == 0)
    def _():
        run_ref[...] = jnp.zeros_like(run_ref)

    tm = e1.shape[1]
    hot1 = jnp.concatenate([sel[k] * t1[m] for k in range(N_GROUPS) for m in range(EXP_PER_GROUP)], axis=0)
    hot2 = jnp.concatenate([sel[k] * t2[m] for k in range(N_GROUPS) for m in range(EXP_PER_GROUP)], axis=0)
    earlier = (lax.broadcasted_iota(jnp.int32, (tm, tm), 0) < lax.broadcasted_iota(jnp.int32, (tm, tm), 1))
    earlier = jnp.where(earlier, 1.0, 0.0).astype(BF16)
    pre1 = _dot(hot1.astype(BF16), earlier)
    pre2 = _dot(hot2.astype(BF16), earlier)
    tot1 = jnp.sum(hot1, axis=1, keepdims=True)
    tot2 = jnp.sum(hot2, axis=1, keepdims=True)
    base = run_ref[:, 0:1]
    rank1 = jnp.sum(hot1 * (base + pre1), axis=0, keepdims=True)
    rank2 = jnp.sum(hot2 * (base + tot1 + pre2), axis=0, keepdims=True)
    run_ref[...] = run_ref[...] + (tot1 + tot2)
    cnt_ref[...] = run_ref[...]

    info_ref[...] = jnp.zeros_like(info_ref)
    info_ref[0:1, :] = e1
    info_ref[1:2, :] = e2
    info_ref[2:3, :] = p1 * scale
    info_ref[3:4, :] = p2 * scale
    info_ref[4:5, :] = rank1
    info_ref[5:6, :] = rank2


def _moe_route(x, g, w_hi, w_lo, b, tm):
    T, D = x.shape
    R = w_hi.shape[0]
    return pl.pallas_call(
        _moe_route_kernel,
        grid=(T // tm,),
        in_specs=[pl.BlockSpec((tm, D), lambda i: (i, 0)),
                  pl.BlockSpec((1, D), lambda i: (0, 0)),
                  pl.BlockSpec((R, D), lambda i: (0, 0)),
                  pl.BlockSpec((R, D), lambda i: (0, 0)),
                  pl.BlockSpec((R, 1), lambda i: (0, 0))],
        out_specs=[pl.BlockSpec((SUBLANES, tm), lambda i: (0, i)),
                   pl.BlockSpec((N_EXPERTS, LANES), lambda i: (0, 0))],
        out_shape=[jax.ShapeDtypeStruct((SUBLANES, T), F32),
                   jax.ShapeDtypeStruct((N_EXPERTS, LANES), F32)],
        scratch_shapes=[pltpu.VMEM((N_EXPERTS, LANES), F32)],
        compiler_params=_cparams("arbitrary"),
        name="moe_route",
    )(x, g, w_hi, w_lo, b)


def _moe_plan(info, counts, tr):
    T = info.shape[1]
    P = 2 * T
    n_tiles = P // tr + N_EXPERTS
    experts = jnp.arange(N_EXPERTS, dtype=jnp.int32)
    eid = info[0:2].astype(jnp.int32).reshape(P)
    rank = info[4:6].astype(jnp.int32).reshape(P)
    tok = jnp.tile(jnp.arange(T, dtype=jnp.int32), 2)
    counts = counts[:, 0].astype(jnp.int32)
    padded = ((counts + tr - 1) // tr) * tr
    ends = jnp.cumsum(padded)
    starts = ends - padded
    slot = jnp.sum(jnp.where(eid[:, None] == experts[None, :], starts[None, :], 0), axis=1) + rank
    src_tok = jnp.zeros((n_tiles * tr,), jnp.int32).at[slot].set(tok)
    n_used = ends[-1] // tr
    tile_row = jnp.minimum(jnp.arange(n_tiles, dtype=jnp.int32), n_used - 1) * tr
    tile_expert = jnp.sum((tile_row[:, None] >= ends[None, :]).astype(jnp.int32), axis=1)
    return src_tok, tile_expert, n_used.reshape(1), slot


def _moe_expert_kernel(tr, src_ref, texp_ref, nused_ref, x_hbm, g_ref, wg_ref, wu_ref, wd_ref, o_ref,
                       xbuf, wg_b, wu_b, wd_b, sem):
    i = pl.program_id(0)
    n_used = nused_ref[0]

    def start_gather(tile, buf):
        def body(r, c):
            tok = src_ref[tile * tr + r]
            pltpu.make_async_copy(x_hbm.at[pl.ds(tok, 1)], xbuf.at[buf, pl.ds(r, 1)], sem.at[buf]).start()
            return c
        lax.fori_loop(0, tr, body, 0, unroll=8)

    def wait_gather(buf):
        pltpu.make_async_copy(x_hbm.at[pl.ds(0, tr)], xbuf.at[buf], sem.at[buf]).wait()

    @pl.when(i == 0)
    def _():
        start_gather(0, 0)

    @pl.when(i + 1 < n_used)
    def _():
        start_gather(i + 1, (i + 1) % 2)

    @pl.when((i == 0) | (texp_ref[i] != texp_ref[jnp.maximum(i - 1, 0)]))
    def _():
        wg_b[...] = wg_ref[0].astype(BF16)
        wu_b[...] = wu_ref[0].astype(BF16)
        wd_b[...] = wd_ref[0].astype(BF16)

    @pl.when(i < n_used)
    def _():
        buf = i % 2
        wait_gather(buf)
        h = _rms(xbuf[buf], g_ref[...]).astype(BF16)
        hg = _dot(h, wg_b[...])
        hu = _dot(h, wu_b[...])
        act = hg * _sigmoid(hg) * hu
        o_ref[...] = _dot(act.astype(BF16), wd_b[...])

    @pl.when(i >= n_used)
    def _():
        o_ref[...] = jnp.zeros_like(o_ref)


def _moe_experts(x, g, src_tok, tile_expert, n_used, w_gate, w_up, w_down, layer, tr):
    T, D = x.shape
    Fd = w_gate.shape[3]
    n_tiles = src_tok.shape[0] // tr
    grid_spec = pltpu.PrefetchScalarGridSpec(
        num_scalar_prefetch=3,
        grid=(n_tiles,),
        in_specs=[pl.BlockSpec(memory_space=pl.ANY),
                  pl.BlockSpec((1, D), lambda i, s, e, n: (0, 0)),
                  pl.BlockSpec((None, 1, D, Fd), lambda i, s, e, n: (layer, e[i], 0, 0)),
                  pl.BlockSpec((None, 1, D, Fd), lambda i, s, e, n: (layer, e[i], 0, 0)),
                  pl.BlockSpec((None, 1, Fd, D), lambda i, s, e, n: (layer, e[i], 0, 0))],
        out_specs=pl.BlockSpec((tr, D), lambda i, s, e, n: (i, 0)),
        scratch_shapes=[pltpu.VMEM((2, tr, D), F32), pltpu.VMEM((D, Fd), BF16), pltpu.VMEM((D, Fd), BF16),
                        pltpu.VMEM((Fd, D), BF16), pltpu.SemaphoreType.DMA((2,))],
    )
    return pl.pallas_call(
        functools.partial(_moe_expert_kernel, tr),
        grid_spec=grid_spec,
        out_shape=jax.ShapeDtypeStruct((n_tiles * tr, D), F32),
        compiler_params=_cparams("arbitrary"),
        name="moe_experts",
    )(src_tok, tile_expert, n_used, x, g, w_gate, w_up, w_down)


def _moe_combine_kernel(tm, slot_ref, info_ref, x_ref, ys_hbm, o_ref, ybuf, sem):
    i = pl.program_id(0)
    n = pl.num_programs(0)
    T = n * tm

    def start_gather(tile, buf):
        def body(r, c):
            for k in range(2):
                s = slot_ref[k * T + tile * tm + r]
                pltpu.make_async_copy(ys_hbm.at[pl.ds(s, 1)], ybuf.at[buf, k, pl.ds(r, 1)], sem.at[buf]).start()
            return c
        lax.fori_loop(0, tm, body, 0, unroll=4)

    def wait_gather(buf):
        for k in range(2):
            pltpu.make_async_copy(ys_hbm.at[pl.ds(0, tm)], ybuf.at[buf, k], sem.at[buf]).wait()

    @pl.when(i == 0)
    def _():
        start_gather(0, 0)

    @pl.when(i + 1 < n)
    def _():
        start_gather(i + 1, (i + 1) % 2)

    buf = i % 2
    wait_gather(buf)
    gates = info_ref[...].T
    o_ref[...] = x_ref[...] + gates[:, 2:3] * ybuf[buf, 0] + gates[:, 3:4] * ybuf[buf, 1]


def _moe_combine(x, info, slot, ys, tm):
    T, D = x.shape
    grid_spec = pltpu.PrefetchScalarGridSpec(
        num_scalar_prefetch=1,
        grid=(T // tm,),
        in_specs=[pl.BlockSpec((SUBLANES, tm), lambda i, s: (0, i)),
                  pl.BlockSpec((tm, D), lambda i, s: (i, 0)),
                  pl.BlockSpec(memory_space=pl.ANY)],
        out_specs=pl.BlockSpec((tm, D), lambda i, s: (i, 0)),
        scratch_shapes=[pltpu.VMEM((2, 2, tm, D), F32), pltpu.SemaphoreType.DMA((2,))],
    )
    return pl.pallas_call(
        functools.partial(_moe_combine_kernel, tm),
        grid_spec=grid_spec,
        out_shape=jax.ShapeDtypeStruct((T, D), F32),
        compiler_params=_cparams("arbitrary"),
        name="moe_combine",
    )(slot, info, x, ys)


def _ple_kernel(x_ref, xr_ref, p_ref, g_ref, wg_ref, wp_ref, o_ref, h_ref):
    def run(first):
        for rows in _row_chunks(x_ref.shape[0]):
            if first:
                h = _rms(x_ref[rows, :], g_ref[...]).astype(BF16)
                h_ref[rows, :] = h
            else:
                h = h_ref[rows, :]
            gate = _sigmoid(_dot(h, wg_ref[...]))
            o_ref[rows, :] = xr_ref[rows, :] + gate * _dot(p_ref[rows, :].astype(BF16), wp_ref[...])

    pl.when(pl.program_id(1) == 0)(lambda: run(True))
    pl.when(pl.program_id(1) > 0)(lambda: run(False))


def _ple(x, p, g, w_gate, w_proj, layer, tm, tn):
    T, D = x.shape
    P = p.shape[2]
    return pl.pallas_call(
        _ple_kernel,
        grid=(T // tm, D // tn),
        in_specs=[pl.BlockSpec((tm, D), lambda i, n: (i, 0)),
                  pl.BlockSpec((tm, tn), lambda i, n: (i, n)),
                  pl.BlockSpec((None, tm, P), lambda i, n: (layer, i, 0)),
                  pl.BlockSpec((1, D), lambda i, n: (0, 0)),
                  pl.BlockSpec((None, D, tn), lambda i, n: (layer, 0, n)),
                  pl.BlockSpec((None, P, tn), lambda i, n: (layer, 0, n))],
        out_specs=pl.BlockSpec((tm, tn), lambda i, n: (i, n)),
        out_shape=jax.ShapeDtypeStruct((T, D), F32),
        scratch_shapes=[pltpu.VMEM((tm, D), BF16)],
        compiler_params=_cparams("arbitrary", "arbitrary"),
        name="ple",
    )(x, x, p, g, w_gate, w_proj)


def _final_norm_kernel(x_ref, g_ref, o_ref):
    o_ref[...] = _rms(x_ref[...], g_ref[...])


def _final_norm(x, g, tm):
    T, D = x.shape
    return pl.pallas_call(
        _final_norm_kernel,
        grid=(T // tm,),
        in_specs=[pl.BlockSpec((tm, D), lambda i: (i, 0)), pl.BlockSpec((1, D), lambda i: (0, 0))],
        out_specs=pl.BlockSpec((tm, D), lambda i: (i, 0)),
        out_shape=jax.ShapeDtypeStruct((T, D), F32),
        compiler_params=_cparams("arbitrary"),
        name="final_norm",
    )(x, g)


def _chain_tile(p, H):
    t = p.reshape(H // 2, 2, RW_HEAD_DIM).transpose(2, 1, 0)
    bg = LANES // H
    return jnp.broadcast_to(t[:, :, None, :], (RW_HEAD_DIM, 2, bg, H // 2)).reshape(RW_HEAD_DIM, LANES)


def _trunk(x, p, conv0, shift0, wkv0, W, prompt):
    B, L, D = x.shape
    depth = p.shape[0]
    T = B * L
    d_a = W["vn_g"].shape[1]
    H = D // RW_HEAD_DIM
    lc = min(L, CHUNK)
    x = x.reshape(T, D)
    p = p.reshape(depth, T, -1)
    tm = 512
    tm_mix = 256
    tr = 256 if T >= 4096 else 128
    chunk_v, conv_new, shift_new, wkv_stack = [], [], [], None
    for i in range(depth):
        j = i // 2
        if i % 2 == 0:
            proj = _even_inproj(x, W["g_mix"][i][None], W["w_in"], j, d_a, tm, d_a)
            ws_eff = jnp.tile(W["w_s"][j][:, :lc, :lc], (1, CHUNK // lc, CHUNK // lc))
            bias = jnp.tile(jnp.repeat(W["b_s"][j][:, :lc].T, A_GROUP_DIM, axis=1), (CHUNK // lc, 1))
            st = conv0[j]
            if prompt:
                b1 = b2 = st
            else:
                zero = jnp.zeros((B, L, st.shape[-1]), F32)
                b1 = zero.at[:, 0].set(st[:, 1]).reshape(T, -1)
                b2 = zero.at[:, 0].set(st[:, 0]).at[:, 1].set(st[:, 1]).reshape(T, -1)
            x, v_all, cx_all = _even_mix(proj, x, W["vn_g"][j][None], W["vn_b"][j][None], ws_eff, bias,
                                         W["conv_w"][j], W["w_out"], j, b1, b2, prompt, L, tm_mix, d_a)
            start = ((L - 1) // CHUNK) * CHUNK
            chunk_v.append(v_all.reshape(B, L, -1)[:, start:])
            conv_new.append(cx_all.reshape(B, L, -1)[:, L - 2:])
        else:
            st = shift0[j]
            if prompt:
                bnd = st[:, None, :]
            else:
                bnd = jnp.zeros((B, L, D), F32).at[:, 0].set(st).reshape(T, D)
            xrkv, xw, xa, xg, h_all = _rwkv_prep(x, W["g_mix"][i][None], W["rw_mu"][j], bnd, prompt, L, tm_mix)
            rkv = _bmm(xrkv, W["rw_w_rkv"], j, tm, B, L)
            decay = _lora(xw, W["rw_w1"][j], W["rw_w2"][j], W["rw_w0"][j][None], "tanh", "decay", tm, (B, L))
            a = _lora(xa, W["rw_a1"][j], W["rw_a2"][j], W["rw_a0"][j][None], "none", "sigmoid", tm, (B, L))
            g = _lora(xg, W["rw_g1"][j], W["rw_g2"][j], jnp.zeros((1, D), F32), "sigmoid", "none", tm)
            consts = (_chain_tile(W["rw_k_k"][j], H), _chain_tile(W["rw_k_a"][j], H),
                      _chain_tile(W["rw_r_k"][j].reshape(-1), H),
                      _chain_tile(W["rw_gn_g"][j], H), _chain_tile(W["rw_gn_b"][j], H))
            z, wkv_stack = _wkv_scan(rkv, decay, a, wkv0, j, consts, min(L, 32), wkv_stack)
            x = _rwkv_out(z, g, x, W["rw_w_o"], j, tm_mix, L)
            shift_new.append(h_all.reshape(B, L, D)[:, -1])
        info, counts = _moe_route(x, W["g_ffn"][i][None], W["moe_r_hi"][i], W["moe_r_lo"][i], W["moe_r_b"][i], tm)
        src_tok, tile_expert, n_used, slot = _moe_plan(info, counts, tr)
        ys = _moe_experts(x, W["g_ffn"][i][None], src_tok, tile_expert, n_used,
                          W["moe_w_gate"], W["moe_w_up"], W["moe_w_down"], i, tr)
        x = _moe_combine(x, info, slot, ys, tm_mix)
        x = _ple(x, p, W["ple_g"][i][None], W["ple_w_gate"], W["ple_w_proj"], i, tm, 1024)
    y = _final_norm(x, W["g_final"][None], tm).reshape(B, L, D)
    return y, jnp.stack(chunk_v), jnp.stack(conv_new), jnp.stack(shift_new), wkv_stack


def _pad_rank(w1, w2):
    r = w1.shape[-1]
    rp = -(-r // LANES) * LANES
    w1 = jnp.pad(w1, ((0, 0), (0, 0), (0, rp - r)))
    w2 = jnp.pad(w2, ((0, 0), (0, rp - r), (0, 0)))
    return w1.astype(BF16), w2.astype(BF16)


def kernel(x_prompt, x_sample, state_conv, state_shift, state_wkv, p_prompt, p_sample, g_mix, g_ffn, g_final, w_in_even, vn_g, vn_b, w_s, b_s, conv_w, w_out_even, rw_mu, rw_w_rkv, rw_w0, rw_w1, rw_w2, rw_a0, rw_a1, rw_a2, rw_g1, rw_g2, rw_k_k, rw_k_a, rw_r_k, rw_gn_g, rw_gn_b, rw_w_o, moe_w_gr, moe_b_gr, moe_w_er, moe_b_er, moe_w_gate, moe_w_up, moe_w_down, ple_g, ple_w_gate, ple_w_proj):
    depth, D = g_mix.shape
    r_w = jnp.zeros((depth, 32, D), F32)
    r_w = r_w.at[:, :N_GROUPS].set(jnp.swapaxes(moe_w_gr, 1, 2))
    r_w = r_w.at[:, SUBLANES:SUBLANES + N_EXPERTS].set(jnp.swapaxes(moe_w_er, 1, 2))
    r_hi = r_w.astype(BF16)
    r_lo = (r_w - r_hi.astype(F32)).astype(BF16)
    r_b = jnp.zeros((depth, 32, 1), F32)
    r_b = r_b.at[:, :N_GROUPS, 0].set(moe_b_gr).at[:, SUBLANES:SUBLANES + N_EXPERTS, 0].set(moe_b_er)
    w1, w2 = _pad_rank(rw_w1, rw_w2)
    a1, a2 = _pad_rank(rw_a1, rw_a2)
    g1, g2 = _pad_rank(rw_g1, rw_g2)
    W = dict(g_mix=g_mix, g_ffn=g_ffn, g_final=g_final,
             w_in=w_in_even.astype(BF16), vn_g=vn_g, vn_b=vn_b, w_s=w_s, b_s=b_s, conv_w=conv_w,
             w_out=w_out_even.astype(BF16),
             rw_mu=rw_mu, rw_w_rkv=rw_w_rkv.astype(BF16), rw_w0=rw_w0, rw_w1=w1, rw_w2=w2,
             rw_a0=rw_a0, rw_a1=a1, rw_a2=a2, rw_g1=g1, rw_g2=g2,
             rw_k_k=rw_k_k, rw_k_a=rw_k_a, rw_r_k=rw_r_k, rw_gn_g=rw_gn_g, rw_gn_b=rw_gn_b,
             rw_w_o=rw_w_o.astype(BF16),
             moe_r_hi=r_hi, moe_r_lo=r_lo, moe_r_b=r_b,
             moe_w_gate=moe_w_gate, moe_w_up=moe_w_up, moe_w_down=moe_w_down,
             ple_g=ple_g, ple_w_gate=ple_w_gate.astype(BF16), ple_w_proj=ple_w_proj.astype(BF16))
    bp = x_prompt.shape[0]
    n_even, n_odd = state_conv.shape[0], state_shift.shape[0]
    conv0 = jnp.zeros((n_even, bp) + state_conv.shape[2:], F32)
    shift0 = jnp.zeros((n_odd, bp, D), F32)
    wkv0 = jnp.zeros((n_odd, bp) + state_wkv.shape[2:], F32)
    y_p, cv_p, conv_p, shift_p, wkv_p = _trunk(x_prompt, p_prompt, conv0, shift0, wkv0, W, True)
    y_s, cv_s, conv_s, shift_s, wkv_s = _trunk(x_sample, p_sample, state_conv, state_shift, state_wkv, W, False)
    return (y_p, y_s, cv_p, conv_p, shift_p, wkv_p, cv_s, conv_s, shift_s, wkv_s)
```

```python
import functools

import jax
import jax.numpy as jnp
from jax import lax
from jax.experimental import pallas as pl
from jax.experimental.pallas import tpu as pltpu

F32 = jnp.float32
BF16 = jnp.bfloat16

NORM_EPS = 1e-6
LN_EPS = 1e-5
RW_GN_EPS = 64e-5
CHUNK = 128
A_GROUP_DIM = 128
RW_HEAD_DIM = 64
N_GROUPS = 4
EXP_PER_GROUP = 4
N_EXPERTS = N_GROUPS * EXP_PER_GROUP

LANES = 128
SUBLANES = 8
VMEM_LIMIT = 52 * 1024 * 1024


def _cparams(*sem):
    return pltpu.CompilerParams(dimension_semantics=sem, vmem_limit_bytes=VMEM_LIMIT)


def _rms(x, g):
    return x * lax.rsqrt(jnp.mean(x * x, axis=-1, keepdims=True) + NORM_EPS) * g


def _gelu_tanh(x):
    return 0.5 * x * (1.0 + jnp.tanh(0.7978845608028654 * (x + 0.044715 * (x * x * x))))


def _sigmoid(x):
    return 1.0 / (1.0 + jnp.exp(-x))


def _dot(a, b):
    return jnp.dot(a, b, preferred_element_type=F32)


ROW_CHUNK = 256


def _row_chunks(tm):
    ch = min(tm, ROW_CHUNK)
    return [slice(c * ch, (c + 1) * ch) for c in range(tm // ch)]


def _even_inproj_kernel(n_gelu, x_ref, g_ref, w_ref, o_ref, h_ref):
    n = pl.program_id(1)

    def run(first, gelu):
        for rows in _row_chunks(x_ref.shape[0]):
            if first:
                h = _rms(x_ref[rows, :], g_ref[...]).astype(BF16)
                h_ref[rows, :] = h
            else:
                h = h_ref[rows, :]
            acc = _dot(h, w_ref[...])
            o_ref[rows, :] = _gelu_tanh(acc) if gelu else acc

    pl.when(n == 0)(lambda: run(True, n_gelu > 0))
    pl.when((n > 0) & (n < n_gelu))(lambda: run(False, True))
    pl.when((n > 0) & (n >= n_gelu))(lambda: run(False, False))


def _even_inproj(x, g, w, layer, d_a, tm, tn):
    T, D = x.shape
    N = w.shape[2]
    return pl.pallas_call(
        functools.partial(_even_inproj_kernel, (2 * d_a) // tn),
        grid=(T // tm, N // tn),
        in_specs=[pl.BlockSpec((tm, D), lambda i, n: (i, 0)),
                  pl.BlockSpec((1, D), lambda i, n: (0, 0)),
                  pl.BlockSpec((None, D, tn), lambda i, n: (layer, 0, n))],
        out_specs=pl.BlockSpec((tm, tn), lambda i, n: (i, n)),
        out_shape=jax.ShapeDtypeStruct((T, N), F32),
        scratch_shapes=[pltpu.VMEM((tm, D), BF16)],
        compiler_params=_cparams("arbitrary", "arbitrary"),
        name="even_inproj",
    )(x, g, w)


def _even_mix_kernel(prompt, seq_len, tiles_per_seq, tm, d_a,
                     u_ref, v_ref, xi_ref, gb_ref, gc_ref, x_ref, vng_ref, vnb_ref,
                     ws_ref, bias_ref, cw_ref, wout_ref, b1_ref, b2_ref,
                     xo_ref, vo_ref, cxo_ref, y_ref, carry_ref):
    i = pl.program_id(0)
    lc = min(seq_len, CHUNK)

    vg = v_ref[...]
    mu = jnp.mean(vg, axis=-1, keepdims=True)
    vc = vg - mu
    var = jnp.mean(vc * vc, axis=-1, keepdims=True)
    vln = vc * lax.rsqrt(var + LN_EPS) * vng_ref[...] + vnb_ref[...]
    vo_ref[...] = vln

    r = lax.broadcasted_iota(jnp.int32, (CHUNK, CHUNK), 0)
    c = lax.broadcasted_iota(jnp.int32, (CHUNK, CHUNK), 1)
    keep = (r >= c) & ((r // lc) == (c // lc))
    n_groups = d_a // A_GROUP_DIM
    for g in range(n_groups):
        wg = jnp.where(keep, ws_ref[g], 0.0).astype(BF16)
        cols = slice(g * A_GROUP_DIM, (g + 1) * A_GROUP_DIM)
        for ch in range(tm // CHUNK):
            rows = slice(ch * CHUNK, (ch + 1) * CHUNK)
            mixed = _dot(wg, vln[rows, cols].astype(BF16)) + bias_ref[:, cols]
            y_ref[rows, cols] = (u_ref[rows, cols] * mixed).astype(BF16)

    cx = gc_ref[...] * xi_ref[...]
    cxo_ref[...] = cx
    row = lax.broadcasted_iota(jnp.int32, cx.shape, 0)
    r1 = pltpu.roll(cx, 1, 0)
    r2 = pltpu.roll(cx, 2, 0)
    if prompt:
        @pl.when(i % tiles_per_seq == 0)
        def _():
            carry_ref[...] = jnp.zeros_like(carry_ref)
            carry_ref[SUBLANES - 2:SUBLANES, :] = b1_ref[0]
        p1 = carry_ref[SUBLANES - 1:SUBLANES, :]
        p2 = carry_ref[SUBLANES - 2:SUBLANES - 1, :]
        s1 = jnp.where(row == 0, p1, r1)
        s2 = jnp.where(row == 0, p2, jnp.where(row == 1, p1, r2))
        carry_ref[...] = cx[tm - SUBLANES:tm, :]
    else:
        pos = row % seq_len
        s1 = jnp.where(pos == 0, b1_ref[...], r1)
        s2 = jnp.where(pos <= 1, b2_ref[...], r2)
    conv = s2 * cw_ref[0:1, :] + s1 * cw_ref[1:2, :] + cx * cw_ref[2:3, :]
    y_ref[:, d_a:] = (gb_ref[...] * conv).astype(BF16)

    xo_ref[...] = x_ref[...] + _dot(y_ref[...], wout_ref[...])


def _even_mix(proj, x, vn_g, vn_b, ws_eff, bias, conv_w, w_out, layer, b1, b2, prompt, seq_len, tm, d_a):
    T, D = x.shape
    d_b = D - d_a
    tps = max(seq_len // tm, 1)
    col = lambda k: pl.BlockSpec((tm, d_a), lambda i, k=k: (i, k))
    const = lambda shape: pl.BlockSpec(shape, lambda i: (0,) * len(shape))
    if prompt:
        b1_spec = pl.BlockSpec((1, 2, d_b), lambda i: (i // tps, 0, 0))
        b2_spec = pl.BlockSpec((1, 2, d_b), lambda i: (i // tps, 0, 0))
    else:
        b1_spec = pl.BlockSpec((tm, d_b), lambda i: (i, 0))
        b2_spec = pl.BlockSpec((tm, d_b), lambda i: (i, 0))
    return pl.pallas_call(
        functools.partial(_even_mix_kernel, prompt, seq_len, tps, tm, d_a),
        grid=(T // tm,),
        in_specs=[col(0), col(1), col(2), col(3), col(4),
                  pl.BlockSpec((tm, D), lambda i: (i, 0)),
                  const((1, d_a)), const((1, d_a)),
                  const(ws_eff.shape), const(bias.shape), const(conv_w.shape),
                  pl.BlockSpec((None,) + w_out.shape[1:], lambda i: (layer, 0, 0)),
                  b1_spec, b2_spec],
        out_specs=[pl.BlockSpec((tm, D), lambda i: (i, 0)),
                   pl.BlockSpec((tm, d_a), lambda i: (i, 0)),
                   pl.BlockSpec((tm, d_b), lambda i: (i, 0))],
        out_shape=[jax.ShapeDtypeStruct((T, D), F32),
                   jax.ShapeDtypeStruct((T, d_a), F32),
                   jax.ShapeDtypeStruct((T, d_b), F32)],
        scratch_shapes=[pltpu.VMEM((tm, D), BF16), pltpu.VMEM((SUBLANES, d_b), F32)],
        compiler_params=_cparams("arbitrary"),
        name="even_mix",
    )(proj, proj, proj, proj, proj, x, vn_g, vn_b, ws_eff, bias, conv_w, w_out, b1, b2)


def _rwkv_prep_kernel(prompt, seq_len, tiles_per_seq, tm,
                      x_ref, g_ref, mu_ref, b_ref,
                      rkv_ref, xw_ref, xa_ref, xg_ref, h_ref, carry_ref):
    i = pl.program_id(0)
    h = _rms(x_ref[...], g_ref[...])
    h_ref[...] = h
    row = lax.broadcasted_iota(jnp.int32, h.shape, 0)
    r1 = pltpu.roll(h, 1, 0)
    if prompt:
        @pl.when(i % tiles_per_seq == 0)
        def _():
            carry_ref[...] = jnp.zeros_like(carry_ref)
            carry_ref[SUBLANES - 1:SUBLANES, :] = b_ref[0]
        prev = jnp.where(row == 0, carry_ref[SUBLANES - 1:SUBLANES, :], r1)
        carry_ref[...] = h[tm - SUBLANES:tm, :]
    else:
        prev = jnp.where(row % seq_len == 0, b_ref[...], r1)
    xx = prev - h
    rkv_ref[0] = (h + xx * mu_ref[0:1, :]).astype(BF16)
    xw_ref[...] = (h + xx * mu_ref[1:2, :]).astype(BF16)
    rkv_ref[1] = (h + xx * mu_ref[2:3, :]).astype(BF16)
    rkv_ref[2] = (h + xx * mu_ref[3:4, :]).astype(BF16)
    xa_ref[...] = (h + xx * mu_ref[4:5, :]).astype(BF16)
    xg_ref[...] = (h + xx * mu_ref[5:6, :]).astype(BF16)


def _rwkv_prep(x, g, mu, bnd, prompt, seq_len, tm):
    T, D = x.shape
    tps = max(seq_len // tm, 1)
    if prompt:
        b_spec = pl.BlockSpec((1, 1, D), lambda i: (i // tps, 0, 0))
    else:
        b_spec = pl.BlockSpec((tm, D), lambda i: (i, 0))
    tok = pl.BlockSpec((tm, D), lambda i: (i, 0))
    return pl.pallas_call(
        functools.partial(_rwkv_prep_kernel, prompt, seq_len, tps, tm),
        grid=(T // tm,),
        in_specs=[tok, pl.BlockSpec((1, D), lambda i: (0, 0)),
                  pl.BlockSpec(mu.shape, lambda i: (0, 0)), b_spec],
        out_specs=[pl.BlockSpec((3, tm, D), lambda i: (0, i, 0)), tok, tok, tok, tok],
        out_shape=[jax.ShapeDtypeStruct((3, T, D), BF16),
                   jax.ShapeDtypeStruct((T, D), BF16),
                   jax.ShapeDtypeStruct((T, D), BF16),
                   jax.ShapeDtypeStruct((T, D), BF16),
                   jax.ShapeDtypeStruct((T, D), F32)],
        scratch_shapes=[pltpu.VMEM((SUBLANES, D), F32)],
        compiler_params=_cparams("arbitrary"),
        name="rwkv_prep",
    )(x, g, mu, bnd)


HEAD_PAIRS = 16
GROUP_BATCH = 4


def _store_scan_rows(o_ref, lead, acc, nb, lt, row0=0):
    rows = acc.shape[0]
    if nb == 1:
        o_ref[lead + (0, slice(row0, row0 + rows))] = acc.reshape(rows, HEAD_PAIRS, LANES)
    else:
        per_group = GROUP_BATCH * lt
        g0, ng = row0 // per_group, rows // per_group
        a5 = acc.reshape(ng, GROUP_BATCH, lt, HEAD_PAIRS, LANES)
        for bl in range(GROUP_BATCH):
            o_ref[lead + (slice(g0, g0 + ng), slice(None), slice(bl * HEAD_PAIRS, (bl + 1) * HEAD_PAIRS))] = a5[:, bl]


def _load_scan_rows(z_ref, nb, lt):
    if nb == 1:
        return z_ref[0].reshape(lt, HEAD_PAIRS * LANES)
    parts = [z_ref[:, :, bl * HEAD_PAIRS:(bl + 1) * HEAD_PAIRS, :] for bl in range(GROUP_BATCH)]
    return jnp.stack(parts, axis=1).reshape(nb * lt, HEAD_PAIRS * LANES)


def _scan_rows_spec(nb, lt, tps, lead=()):
    nl = len(lead)
    if nb == 1:
        shape = (1,) * nl + (1, lt, HEAD_PAIRS, LANES)
        return shape, lambda i: (i // tps // GROUP_BATCH, i % tps, (i // tps) % GROUP_BATCH, 0)
    shape = (1,) * nl + (nb // GROUP_BATCH, lt, GROUP_BATCH * HEAD_PAIRS, LANES)
    return shape, lambda i: (i, 0, 0, 0)


def _tile_geometry(tm, seq_len):
    lt = min(tm, seq_len)
    return tm // lt, lt, seq_len // lt


def _bmm_kernel(nb, lt, x_ref, w_ref, o_ref):
    _store_scan_rows(o_ref, (0,), _dot(x_ref[0], w_ref[0]), nb, lt)


def _bmm(x, w, layer, tm, B, L):
    n3, T, D = x.shape
    N = w.shape[3]
    nb, lt, tps = _tile_geometry(tm, L)
    shape, idx = _scan_rows_spec(nb, lt, tps, lead=(0,))
    return pl.pallas_call(
        functools.partial(_bmm_kernel, nb, lt),
        grid=(n3, T // tm),
        in_specs=[pl.BlockSpec((1, tm, D), lambda b, i: (b, i, 0)),
                  pl.BlockSpec((None, 1, D, N), lambda b, i: (layer, b, 0, 0))],
        out_specs=pl.BlockSpec(shape, lambda b, i: (b,) + idx(i)),
        out_shape=jax.ShapeDtypeStruct((n3, B // GROUP_BATCH, L, GROUP_BATCH * HEAD_PAIRS, LANES), F32),
        compiler_params=_cparams("arbitrary", "arbitrary"),
        name="rwkv_rkv_proj",
    )(x, w)


def _lora_kernel(mid, out, geom, x_ref, w1_ref, w2_ref, b_ref, o_ref):
    for rows in _row_chunks(x_ref.shape[0]):
        t = _dot(x_ref[rows, :], w1_ref[...])
        if mid == "tanh":
            t = jnp.tanh(t)
        elif mid == "sigmoid":
            t = _sigmoid(t)
        z = _dot(t.astype(BF16), w2_ref[...]) + b_ref[...]
        if out == "decay":
            w_log = jnp.minimum(z, 0.0) - jnp.log(1.0 + jnp.exp(-jnp.abs(z))) - 0.5
            z = jnp.exp(-jnp.exp(w_log))
        elif out == "sigmoid":
            z = _sigmoid(z)
        if geom is None:
            o_ref[rows, :] = z
        else:
            _store_scan_rows(o_ref, (), z, *geom, row0=rows.start)


def _lora(x, w1, w2, b, mid, out, tm, head_pairs=None):
    T, D = x.shape
    R = w1.shape[1]
    if head_pairs is None:
        geom = None
        out_spec = pl.BlockSpec((tm, D), lambda i: (i, 0))
        out_shape = jax.ShapeDtypeStruct((T, D), F32)
    else:
        B, L = head_pairs
        nb, lt, tps = _tile_geometry(tm, L)
        geom = (nb, lt)
        out_spec = pl.BlockSpec(*_scan_rows_spec(nb, lt, tps))
        out_shape = jax.ShapeDtypeStruct((B // GROUP_BATCH, L, GROUP_BATCH * HEAD_PAIRS, LANES), F32)
    return pl.pallas_call(
        functools.partial(_lora_kernel, mid, out, geom),
        grid=(T // tm,),
        in_specs=[pl.BlockSpec((tm, D), lambda i: (i, 0)),
                  pl.BlockSpec((D, R), lambda i: (0, 0)),
                  pl.BlockSpec((R, D), lambda i: (0, 0)),
                  pl.BlockSpec((1, D), lambda i: (0, 0))],
        out_specs=out_spec,
        out_shape=out_shape,
        compiler_params=_cparams("arbitrary"),
        name="rwkv_lora_" + out,
    )(x, w1, w2, b)


def _wkv_scan_kernel(tb, aliased, r_ref, k_ref, v_ref, w_ref, a_ref, s0_ref, kk_c, ka_c, rk_c, gg_c, gb_c, *rest):
    z_ref, st_ref, s_ref, tile_ref, vec_ref, zbuf_ref, slab_ref = rest[1:] if aliased else rest
    n = RW_HEAD_DIM
    half = LANES // 2
    step0 = pl.program_id(1)
    seq_refs = (r_ref, w_ref, k_ref, v_ref, a_ref)
    R, W, K, V, A = range(5)

    nchunk = n * n // LANES
    per_parity = LANES // 2 // GROUP_BATCH
    chain_rows = [pl.ds(b * 2 * per_parity + hl, per_parity, stride=2) for hl in range(2) for b in range(GROUP_BATCH)]

    @pl.when(step0 == 0)
    def _():
        zbuf_ref[...] = jnp.zeros_like(zbuf_ref)
        tile_ref[...] = jnp.zeros_like(tile_ref)
        vec_ref[...] = jnp.zeros_like(vec_ref)
        x2 = s0_ref[...].reshape(LANES, n * n)
        for c in range(nchunk):
            slab_ref[c] = x2[:, c * LANES:(c + 1) * LANES]
        for c in range(nchunk):
            xt = jnp.concatenate([slab_ref[c, rows, :] for rows in chain_rows], axis=0).T
            for il in range(LANES // n):
                s_ref[pl.ds(c * (LANES // n) + il, n, stride=n), :] = xt[il * n:(il + 1) * n]

    def fetch(t, slot):
        for q, ref in enumerate(seq_refs):
            xt = ref[(0,) * (len(ref.shape) - 3) + (t,)].T
            tile_ref[slot, q] = jnp.concatenate([xt[:n], xt[n:]], axis=1)
        k_t = tile_ref[slot, K]
        a_t = tile_ref[slot, A]
        kk = k_t * kk_c[...]
        nrm = jnp.sqrt(jnp.sum(kk * kk, axis=0, keepdims=True))
        kk = kk / jnp.maximum(nrm, 1e-12)
        kmod = k_t * (1.0 + (a_t - 1.0) * ka_c[...])
        vec_ref[slot, 0] = kk
        vec_ref[slot, 1] = kk * a_t
        vec_ref[slot, 2] = kmod
        vec_ref[slot, 3, 0:1, :] = jnp.sum(tile_ref[slot, R] * kmod * rk_c[...], axis=0, keepdims=True)

    def flush(t, slot):
        y = zbuf_ref[slot]
        mu = jnp.mean(y, axis=0, keepdims=True)
        yc = y - mu
        var = jnp.mean(yc * yc, axis=0, keepdims=True)
        yn = yc * lax.rsqrt(var + RW_GN_EPS) * gg_c[...] + gb_c[...]
        z = yn + vec_ref[slot, 3, 0:1, :] * tile_ref[slot, V]
        z_ref[0, t] = jnp.concatenate([z[:, :half], z[:, half:]], axis=0).T

    def recur(slot):
        hn = n // 2
        for i0 in range(0, n, hn):
            v_t = tile_ref[slot, V, i0:i0 + hn, :]
            sa = jnp.zeros((hn, LANES), F32)
            for j in range(n):
                sa = sa + s_ref[j * n + i0:j * n + i0 + hn, :] * vec_ref[slot, 0, j:j + 1, :]
            y = jnp.zeros((hn, LANES), F32)
            for j in range(n):
                s_new = (s_ref[j * n + i0:j * n + i0 + hn, :] * tile_ref[slot, W, j:j + 1, :]
                         - sa * vec_ref[slot, 1, j:j + 1, :] + v_t * vec_ref[slot, 2, j:j + 1, :])
                s_ref[j * n + i0:j * n + i0 + hn, :] = s_new
                y = y + s_new * tile_ref[slot, R, j:j + 1, :]
            zbuf_ref[slot, i0:i0 + hn, :] = y

    fetch(0, 0)

    def pair(u, carry):
        t = 2 * u
        flush(jnp.maximum(t - 1, 0), 1)
        fetch(t + 1, 1)
        recur(0)
        flush(t, 0)
        fetch(jnp.minimum(t + 2, tb - 1), 0)
        recur(1)
        return carry

    lax.fori_loop(0, tb // 2, pair, 0)
    flush(tb - 1, 1)

    @pl.when(step0 == pl.num_programs(1) - 1)
    def _():
        for c in range(nchunk):
            xt = jnp.concatenate([s_ref[pl.ds(c * (LANES // n) + il, n, stride=n), :]
                                  for il in range(LANES // n)], axis=0).T
            for q, rows in enumerate(chain_rows):
                slab_ref[c, rows, :] = xt[q * per_parity:(q + 1) * per_parity]
        x2 = jnp.concatenate([slab_ref[c] for c in range(nchunk)], axis=1)
        st_ref[...] = x2.reshape(st_ref.shape)


def _wkv_scan(rkv, w, a, state, layer, consts, tb, stacked=None):
    _, G, L, half, _ = rkv.shape
    n = RW_HEAD_DIM
    H = state.shape[2]
    assert tb % 2 == 0 and L % tb == 0 and half == LANES // 2 and GROUP_BATCH * H == LANES
    seq3 = lambda m: pl.BlockSpec((1, 1, tb, half, LANES), lambda g, t, m=m: (m, g, t, 0, 0))
    seq = pl.BlockSpec((1, tb, half, LANES), lambda g, t: (g, t, 0, 0))
    st = pl.BlockSpec((None, GROUP_BATCH, H, n, n), lambda g, t: (layer, g, 0, 0, 0))
    cst = pl.BlockSpec((n, LANES), lambda g, t: (0, 0))
    extra, extra_specs, aliases = (), [], {}
    if stacked is not None:
        extra, extra_specs, aliases = (stacked,), [pl.BlockSpec(memory_space=pl.ANY)], {11: 1}
    return pl.pallas_call(
        functools.partial(_wkv_scan_kernel, tb, stacked is not None),
        grid=(G, L // tb),
        in_specs=[seq3(0), seq3(1), seq3(2), seq, seq, st, cst, cst, cst, cst, cst] + extra_specs,
        out_specs=[seq, st],
        out_shape=[jax.ShapeDtypeStruct((G, L, half, LANES), F32),
                   jax.ShapeDtypeStruct(state.shape, F32)],
        scratch_shapes=[pltpu.VMEM((n * n, LANES), F32), pltpu.VMEM((2, 5, n, LANES), F32),
                        pltpu.VMEM((2, 4, n, LANES), F32), pltpu.VMEM((2, n, LANES), F32),
                        pltpu.VMEM((n * n // LANES, LANES, LANES), F32)],
        input_output_aliases=aliases,
        compiler_params=_cparams("arbitrary", "arbitrary"),
        name="rwkv_scan",
    )(rkv, rkv, rkv, w, a, state, *consts, *extra)


def _rwkv_out_kernel(nb, lt, z_ref, g_ref, x_ref, w_ref, o_ref):
    z = _load_scan_rows(z_ref, nb, lt)
    o_ref[...] = x_ref[...] + _dot((z * g_ref[...]).astype(BF16), w_ref[...])


def _rwkv_out(z, g, x, w, layer, tm, L):
    T, D = x.shape
    nb, lt, tps = _tile_geometry(tm, L)
    tok = pl.BlockSpec((tm, D), lambda i: (i, 0))
    return pl.pallas_call(
        functools.partial(_rwkv_out_kernel, nb, lt),
        grid=(T // tm,),
        in_specs=[pl.BlockSpec(*_scan_rows_spec(nb, lt, tps)),
                  tok, tok, pl.BlockSpec((None, D, D), lambda i: (layer, 0, 0))],
        out_specs=tok,
        out_shape=jax.ShapeDtypeStruct((T, D), F32),
        compiler_params=_cparams("arbitrary"),
        name="rwkv_out",
    )(z, g, x, w)


def _moe_route_kernel(x_ref, g_ref, whi_ref, wlo_ref, b_ref, info_ref, cnt_ref, run_ref):
    h = _rms(x_ref[...], g_ref[...])
    h_hi = h.astype(BF16)
    h_lo = (h - h_hi.astype(F32)).astype(BF16)
    nt = (((1,), (1,)), ((), ()))
    logits = (lax.dot_general(whi_ref[...], h_hi, nt, preferred_element_type=F32)
              + lax.dot_general(whi_ref[...], h_lo, nt, preferred_element_type=F32)
              + lax.dot_general(wlo_ref[...], h_hi, nt, preferred_element_type=F32)
              + b_ref[...])
    gl = [logits[k:k + 1, :] for k in range(N_GROUPS)]
    gmax = functools.reduce(jnp.maximum, gl)
    sel, taken = [], jnp.zeros_like(gmax)
    for k in range(N_GROUPS):
        s = jnp.where((gl[k] == gmax) & (taken == 0.0), 1.0, 0.0)
        taken = taken + s
        sel.append(s)
    p_group = 1.0 / functools.reduce(jnp.add, [jnp.exp(x - gmax) for x in gl])

    el = []
    for m in range(EXP_PER_GROUP):
        rows = [logits[SUBLANES + k * EXP_PER_GROUP + m:SUBLANES + k * EXP_PER_GROUP + m + 1, :]
                for k in range(N_GROUPS)]
        el.append(functools.reduce(jnp.add, [jnp.where(sel[k] > 0.0, rows[k], 0.0) for k in range(N_GROUPS)]))
    emax = functools.reduce(jnp.maximum, el)
    ee = [jnp.exp(x - emax) for x in el]
    esum = functools.reduce(jnp.add, ee)
    prob = [x / esum for x in ee]

    def first_argmax(vals):
        vmax = functools.reduce(jnp.maximum, vals)
        hot, used = [], jnp.zeros_like(vmax)
        for x in vals:
            s = jnp.where((x == vmax) & (used == 0.0), 1.0, 0.0)
            used = used + s
            hot.append(s)
        return hot, vmax

    t1, p1 = first_argmax(prob)
    rest = [jnp.where(t1[m] > 0.0, -1.0, prob[m]) for m in range(EXP_PER_GROUP)]
    t2, p2 = first_argmax(rest)
    scale = p_group / (p1 + p2)
    group = functools.reduce(jnp.add, [sel[k] * float(k * EXP_PER_GROUP) for k in range(N_GROUPS)])
    e1 = group + functools.reduce(jnp.add, [t1[m] * float(m) for m in range(EXP_PER_GROUP)])
    e2 = group + functools.reduce(jnp.add, [t2[m] * float(m) for m in range(EXP_PER_GROUP)])

    @pl.when(pl.program_id(0) == 0)
    def _():
        run_ref[...] = jnp.zeros_like(run_ref)

    tm = e1.shape[1]
    hot1 = jnp.concatenate([sel[k] * t1[m] for k in range(N_GROUPS) for m in range(EXP_PER_GROUP)], axis=0)
    hot2 = jnp.concatenate([sel[k] * t2[m] for k in range(N_GROUPS) for m in range(EXP_PER_GROUP)], axis=0)
    earlier = (lax.broadcasted_iota(jnp.int32, (tm, tm), 0) < lax.broadcasted_iota(jnp.int32, (tm, tm), 1))
    earlier = jnp.where(earlier, 1.0, 0.0).astype(BF16)
    pre1 = _dot(hot1.astype(BF16), earlier)
    pre2 = _dot(hot2.astype(BF16), earlier)
    tot1 = jnp.sum(hot1, axis=1, keepdims=True)
    tot2 = jnp.sum(hot2, axis=1, keepdims=True)
    base = run_ref[:, 0:1]
    rank1 = jnp.sum(hot1 * (base + pre1), axis=0, keepdims=True)
    rank2 = jnp.sum(hot2 * (base + tot1 + pre2), axis=0, keepdims=True)
    run_ref[...] = run_ref[...] + (tot1 + tot2)
    cnt_ref[...] = run_ref[...]

    info_ref[...] = jnp.zeros_like(info_ref)
    info_ref[0:1, :] = e1
    info_ref[1:2, :] = e2
    info_ref[2:3, :] = p1 * scale
    info_ref[3:4, :] = p2 * scale
    info_ref[4:5, :] = rank1
    info_ref[5:6, :] = rank2


def _moe_route(x, g, w_hi, w_lo, b, tm):
    T, D = x.shape
    R = w_hi.shape[0]
    return pl.pallas_call(
        _moe_route_kernel,
        grid=(T // tm,),
        in_specs=[pl.BlockSpec((tm, D), lambda i: (i, 0)),
                  pl.BlockSpec((1, D), lambda i: (0, 0)),
                  pl.BlockSpec((R, D), lambda i: (0, 0)),
                  pl.BlockSpec((R, D), lambda i: (0, 0)),
                  pl.BlockSpec((R, 1), lambda i: (0, 0))],
        out_specs=[pl.BlockSpec((SUBLANES, tm), lambda i: (0, i)),
                   pl.BlockSpec((N_EXPERTS, LANES), lambda i: (0, 0))],
        out_shape=[jax.ShapeDtypeStruct((SUBLANES, T), F32),
                   jax.ShapeDtypeStruct((N_EXPERTS, LANES), F32)],
        scratch_shapes=[pltpu.VMEM((N_EXPERTS, LANES), F32)],
        compiler_params=_cparams("arbitrary"),
        name="moe_route",
    )(x, g, w_hi, w_lo, b)


def _moe_plan(info, counts, tr):
    T = info.shape[1]
    P = 2 * T
    n_tiles = P // tr + N_EXPERTS
    experts = jnp.arange(N_EXPERTS, dtype=jnp.int32)
    eid = info[0:2].astype(jnp.int32).reshape(P)
    rank = info[4:6].astype(jnp.int32).reshape(P)
    tok = jnp.tile(jnp.arange(T, dtype=jnp.int32), 2)
    counts = counts[:, 0].astype(jnp.int32)
    padded = ((counts + tr - 1) // tr) * tr
    ends = jnp.cumsum(padded)
    starts = ends - padded
    slot = jnp.sum(jnp.where(eid[:, None] == experts[None, :], starts[None, :], 0), axis=1) + rank
    src_tok = jnp.zeros((n_tiles * tr,), jnp.int32).at[slot].set(tok)
    n_used = ends[-1] // tr
    tile_row = jnp.minimum(jnp.arange(n_tiles, dtype=jnp.int32), n_used - 1) * tr
    tile_expert = jnp.sum((tile_row[:, None] >= ends[None, :]).astype(jnp.int32), axis=1)
    return src_tok, tile_expert, n_used.reshape(1), slot


def _moe_expert_kernel(tr, src_ref, texp_ref, nused_ref, x_hbm, g_ref, wg_ref, wu_ref, wd_ref, o_ref,
                       xbuf, wg_b, wu_b, wd_b, sem):
    i = pl.program_id(0)
    n_used = nused_ref[0]

    def start_gather(tile, buf):
        def body(r, c):
            tok = src_ref[tile * tr + r]
            pltpu.make_async_copy(x_hbm.at[pl.ds(tok, 1)], xbuf.at[buf, pl.ds(r, 1)], sem.at[buf]).start()
            return c
        lax.fori_loop(0, tr, body, 0, unroll=8)

    def wait_gather(buf):
        pltpu.make_async_copy(x_hbm.at[pl.ds(0, tr)], xbuf.at[buf], sem.at[buf]).wait()

    @pl.when(i == 0)
    def _():
        start_gather(0, 0)

    @pl.when(i + 1 < n_used)
    def _():
        start_gather(i + 1, (i + 1) % 2)

    @pl.when((i == 0) | (texp_ref[i] != texp_ref[jnp.maximum(i - 1, 0)]))
    def _():
        wg_b[...] = wg_ref[0].astype(BF16)
        wu_b[...] = wu_ref[0].astype(BF16)
        wd_b[...] = wd_ref[0].astype(BF16)

    @pl.when(i < n_used)
    def _():
        buf = i % 2
        wait_gather(buf)
        h = _rms(xbuf[buf], g_ref[...]).astype(BF16)
        hg = _dot(h, wg_b[...])
        hu = _dot(h, wu_b[...])
        act = hg * _sigmoid(hg) * hu
        o_ref[...] = _dot(act.astype(BF16), wd_b[...])

    @pl.when(i >= n_used)
    def _():
        o_ref[...] = jnp.zeros_like(o_ref)


def _moe_experts(x, g, src_tok, tile_expert, n_used, w_gate, w_up, w_down, layer, tr):
    T, D = x.shape
    Fd = w_gate.shape[3]
    n_tiles = src_tok.shape[0] // tr
    grid_spec = pltpu.PrefetchScalarGridSpec(
        num_scalar_prefetch=3,
        grid=(n_tiles,),
        in_specs=[pl.BlockSpec(memory_space=pl.ANY),
                  pl.BlockSpec((1, D), lambda i, s, e, n: (0, 0)),
                  pl.BlockSpec((None, 1, D, Fd), lambda i, s, e, n: (layer, e[i], 0, 0)),
                  pl.BlockSpec((None, 1, D, Fd), lambda i, s, e, n: (layer, e[i], 0, 0)),
                  pl.BlockSpec((None, 1, Fd, D), lambda i, s, e, n: (layer, e[i], 0, 0))],
        out_specs=pl.BlockSpec((tr, D), lambda i, s, e, n: (i, 0)),
        scratch_shapes=[pltpu.VMEM((2, tr, D), F32), pltpu.VMEM((D, Fd), BF16), pltpu.VMEM((D, Fd), BF16),
                        pltpu.VMEM((Fd, D), BF16), pltpu.SemaphoreType.DMA((2,))],
    )
    return pl.pallas_call(
        functools.partial(_moe_expert_kernel, tr),
        grid_spec=grid_spec,
        out_shape=jax.ShapeDtypeStruct((n_tiles * tr, D), F32),
        compiler_params=_cparams("arbitrary"),
        name="moe_experts",
    )(src_tok, tile_expert, n_used, x, g, w_gate, w_up, w_down)


def _moe_combine_kernel(tm, slot_ref, info_ref, x_ref, ys_hbm, o_ref, ybuf, sem):
    i = pl.program_id(0)
    n = pl.num_programs(0)
    T = n * tm

    def start_gather(tile, buf):
        def body(r, c):
            for k in range(2):
                s = slot_ref[k * T + tile * tm + r]
                pltpu.make_async_copy(ys_hbm.at[pl.ds(s, 1)], ybuf.at[buf, k, pl.ds(r, 1)], sem.at[buf]).start()
            return c
        lax.fori_loop(0, tm, body, 0, unroll=4)

    def wait_gather(buf):
        for k in range(2):
            pltpu.make_async_copy(ys_hbm.at[pl.ds(0, tm)], ybuf.at[buf, k], sem.at[buf]).wait()

    @pl.when(i == 0)
    def _():
        start_gather(0, 0)

    @pl.when(i + 1 < n)
    def _():
        start_gather(i + 1, (i + 1) % 2)

    buf = i % 2
    wait_gather(buf)
    gates = info_ref[...].T
    o_ref[...] = x_ref[...] + gates[:, 2:3] * ybuf[buf, 0] + gates[:, 3:4] * ybuf[buf, 1]


def _moe_combine(x, info, slot, ys, tm):
    T, D = x.shape
    grid_spec = pltpu.PrefetchScalarGridSpec(
        num_scalar_prefetch=1,
        grid=(T // tm,),
        in_specs=[pl.BlockSpec((SUBLANES, tm), lambda i, s: (0, i)),
                  pl.BlockSpec((tm, D), lambda i, s: (i, 0)),
                  pl.BlockSpec(memory_space=pl.ANY)],
        out_specs=pl.BlockSpec((tm, D), lambda i, s: (i, 0)),
        scratch_shapes=[pltpu.VMEM((2, 2, tm, D), F32), pltpu.SemaphoreType.DMA((2,))],
    )
    return pl.pallas_call(
        functools.partial(_moe_combine_kernel, tm),
        grid_spec=grid_spec,
        out_shape=jax.ShapeDtypeStruct((T, D), F32),
        compiler_params=_cparams("arbitrary"),
        name="moe_combine",
    )(slot, info, x, ys)


def _ple_kernel(x_ref, p_ref, g_ref, wg_ref, wp_ref, o_ref):
    for rows in _row_chunks(x_ref.shape[0]):
        x = x_ref[rows, :]
        gate = _sigmoid(_dot(_rms(x, g_ref[...]).astype(BF16), wg_ref[...]))
        o_ref[rows, :] = x + gate * _dot(p_ref[rows, :].astype(BF16), wp_ref[...])


def _ple(x, p, g, w_gate, w_proj, layer, tm):
    T, D = x.shape
    P = p.shape[2]
    return pl.pallas_call(
        _ple_kernel,
        grid=(T // tm,),
        in_specs=[pl.BlockSpec((tm, D), lambda i: (i, 0)),
                  pl.BlockSpec((None, tm, P), lambda i: (layer, i, 0)),
                  pl.BlockSpec((1, D), lambda i: (0, 0)),
                  pl.BlockSpec((None, D, D), lambda i: (layer, 0, 0)),
                  pl.BlockSpec((None, P, D), lambda i: (layer, 0, 0))],
        out_specs=pl.BlockSpec((tm, D), lambda i: (i, 0)),
        out_shape=jax.ShapeDtypeStruct((T, D), F32),
        compiler_params=_cparams("arbitrary"),
        name="ple",
    )(x, p, g, w_gate, w_proj)


def _final_norm_kernel(x_ref, g_ref, o_ref):
    o_ref[...] = _rms(x_ref[...], g_ref[...])


def _final_norm(x, g, tm):
    T, D = x.shape
    return pl.pallas_call(
        _final_norm_kernel,
        grid=(T // tm,),
        in_specs=[pl.BlockSpec((tm, D), lambda i: (i, 0)), pl.BlockSpec((1, D), lambda i: (0, 0))],
        out_specs=pl.BlockSpec((tm, D), lambda i: (i, 0)),
        out_shape=jax.ShapeDtypeStruct((T, D), F32),
        compiler_params=_cparams("arbitrary"),
        name="final_norm",
    )(x, g)


def _chain_tile(p, H):
    t = p.reshape(H // 2, 2, RW_HEAD_DIM).transpose(2, 1, 0)
    bg = LANES // H
    return jnp.broadcast_to(t[:, :, None, :], (RW_HEAD_DIM, 2, bg, H // 2)).reshape(RW_HEAD_DIM, LANES)


def _trunk(x, p, conv0, shift0, wkv0, W, prompt):
    B, L, D = x.shape
    depth = p.shape[0]
    T = B * L
    d_a = W["vn_g"].shape[1]
    H = D // RW_HEAD_DIM
    lc = min(L, CHUNK)
    x = x.reshape(T, D)
    p = p.reshape(depth, T, -1)
    tm = 512
    tm_mix = 256
    tr = 256 if T >= 4096 else 128
    chunk_v, conv_new, shift_new, wkv_stack = [], [], [], None
    for i in range(depth):
        j = i // 2
        if i % 2 == 0:
            proj = _even_inproj(x, W["g_mix"][i][None], W["w_in"], j, d_a, min(T, 2 * tm), d_a)
            ws_eff = jnp.tile(W["w_s"][j][:, :lc, :lc], (1, CHUNK // lc, CHUNK // lc))
            bias = jnp.tile(jnp.repeat(W["b_s"][j][:, :lc].T, A_GROUP_DIM, axis=1), (CHUNK // lc, 1))
            st = conv0[j]
            if prompt:
                b1 = b2 = st
            else:
                zero = jnp.zeros((B, L, st.shape[-1]), F32)
                b1 = zero.at[:, 0].set(st[:, 1]).reshape(T, -1)
                b2 = zero.at[:, 0].set(st[:, 0]).at[:, 1].set(st[:, 1]).reshape(T, -1)
            x, v_all, cx_all = _even_mix(proj, x, W["vn_g"][j][None], W["vn_b"][j][None], ws_eff, bias,
                                         W["conv_w"][j], W["w_out"], j, b1, b2, prompt, L, tm_mix, d_a)
            start = ((L - 1) // CHUNK) * CHUNK
            chunk_v.append(v_all.reshape(B, L, -1)[:, start:])
            conv_new.append(cx_all.reshape(B, L, -1)[:, L - 2:])
        else:
            st = shift0[j]
            if prompt:
                bnd = st[:, None, :]
            else:
                bnd = jnp.zeros((B, L, D), F32).at[:, 0].set(st).reshape(T, D)
            xrkv, xw, xa, xg, h_all = _rwkv_prep(x, W["g_mix"][i][None], W["rw_mu"][j], bnd, prompt, L, tm_mix)
            rkv = _bmm(xrkv, W["rw_w_rkv"], j, tm, B, L)
            decay = _lora(xw, W["rw_w1"][j], W["rw_w2"][j], W["rw_w0"][j][None], "tanh", "decay", tm, (B, L))
            a = _lora(xa, W["rw_a1"][j], W["rw_a2"][j], W["rw_a0"][j][None], "none", "sigmoid", tm, (B, L))
            g = _lora(xg, W["rw_g1"][j], W["rw_g2"][j], jnp.zeros((1, D), F32), "sigmoid", "none", tm)
            consts = (_chain_tile(W["rw_k_k"][j], H), _chain_tile(W["rw_k_a"][j], H),
                      _chain_tile(W["rw_r_k"][j].reshape(-1), H),
                      _chain_tile(W["rw_gn_g"][j], H), _chain_tile(W["rw_gn_b"][j], H))
            z, wkv_stack = _wkv_scan(rkv, decay, a, wkv0, j, consts, min(L, 32), wkv_stack)
            x = _rwkv_out(z, g, x, W["rw_w_o"], j, tm_mix, L)
            shift_new.append(h_all.reshape(B, L, D)[:, -1])
        info, counts = _moe_route(x, W["g_ffn"][i][None], W["moe_r_hi"][i], W["moe_r_lo"][i], W["moe_r_b"][i], tm)
        src_tok, tile_expert, n_used, slot = _moe_plan(info, counts, tr)
        ys = _moe_experts(x, W["g_ffn"][i][None], src_tok, tile_expert, n_used,
                          W["moe_w_gate"], W["moe_w_up"], W["moe_w_down"], i, tr)
        x = _moe_combine(x, info, slot, ys, tm_mix)
        x = _ple(x, p, W["ple_g"][i][None], W["ple_w_gate"], W["ple_w_proj"], i, tm)
    y = _final_norm(x, W["g_final"][None], tm).reshape(B, L, D)
    return y, jnp.stack(chunk_v), jnp.stack(conv_new), jnp.stack(shift_new), wkv_stack


def _pad_rank(w1, w2):
    r = w1.shape[-1]
    rp = -(-r // LANES) * LANES
    w1 = jnp.pad(w1, ((0, 0), (0, 0), (0, rp - r)))
    w2 = jnp.pad(w2, ((0, 0), (0, rp - r), (0, 0)))
    return w1.astype(BF16), w2.astype(BF16)


def kernel(x_prompt, x_sample, state_conv, state_shift, state_wkv, p_prompt, p_sample, g_mix, g_ffn, g_final, w_in_even, vn_g, vn_b, w_s, b_s, conv_w, w_out_even, rw_mu, rw_w_rkv, rw_w0, rw_w1, rw_w2, rw_a0, rw_a1, rw_a2, rw_g1, rw_g2, rw_k_k, rw_k_a, rw_r_k, rw_gn_g, rw_gn_b, rw_w_o, moe_w_gr, moe_b_gr, moe_w_er, moe_b_er, moe_w_gate, moe_w_up, moe_w_down, ple_g, ple_w_gate, ple_w_proj):
    depth, D = g_mix.shape
    r_w = jnp.zeros((depth, 32, D), F32)
    r_w = r_w.at[:, :N_GROUPS].set(jnp.swapaxes(moe_w_gr, 1, 2))
    r_w = r_w.at[:, SUBLANES:SUBLANES + N_EXPERTS].set(jnp.swapaxes(moe_w_er, 1, 2))
    r_hi = r_w.astype(BF16)
    r_lo = (r_w - r_hi.astype(F32)).astype(BF16)
    r_b = jnp.zeros((depth, 32, 1), F32)
    r_b = r_b.at[:, :N_GROUPS, 0].set(moe_b_gr).at[:, SUBLANES:SUBLANES + N_EXPERTS, 0].set(moe_b_er)
    w1, w2 = _pad_rank(rw_w1, rw_w2)
    a1, a2 = _pad_rank(rw_a1, rw_a2)
    g1, g2 = _pad_rank(rw_g1, rw_g2)
    W = dict(g_mix=g_mix, g_ffn=g_ffn, g_final=g_final,
             w_in=w_in_even.astype(BF16), vn_g=vn_g, vn_b=vn_b, w_s=w_s, b_s=b_s, conv_w=conv_w,
             w_out=w_out_even.astype(BF16),
             rw_mu=rw_mu, rw_w_rkv=rw_w_rkv.astype(BF16), rw_w0=rw_w0, rw_w1=w1, rw_w2=w2,
             rw_a0=rw_a0, rw_a1=a1, rw_a2=a2, rw_g1=g1, rw_g2=g2,
             rw_k_k=rw_k_k, rw_k_a=rw_k_a, rw_r_k=rw_r_k, rw_gn_g=rw_gn_g, rw_gn_b=rw_gn_b,
             rw_w_o=rw_w_o.astype(BF16),
             moe_r_hi=r_hi, moe_r_lo=r_lo, moe_r_b=r_b,
             moe_w_gate=moe_w_gate, moe_w_up=moe_w_up, moe_w_down=moe_w_down,
             ple_g=ple_g, ple_w_gate=ple_w_gate.astype(BF16), ple_w_proj=ple_w_proj.astype(BF16))
    bp = x_prompt.shape[0]
    n_even, n_odd = state_conv.shape[0], state_shift.shape[0]
    conv0 = jnp.zeros((n_even, bp) + state_conv.shape[2:], F32)
    shift0 = jnp.zeros((n_odd, bp, D), F32)
    wkv0 = jnp.zeros((n_odd, bp) + state_wkv.shape[2:], F32)
    y_p, cv_p, conv_p, shift_p, wkv_p = _trunk(x_prompt, p_prompt, conv0, shift0, wkv0, W, True)
    y_s, cv_s, conv_s, shift_s, wkv_s = _trunk(x_sample, p_sample, state_conv, state_shift, state_wkv, W, False)
    return (y_p, y_s, cv_p, conv_p, shift_p, wkv_p, cv_s, conv_s, shift_s, wkv_s)
```

```python
import functools

import jax
import jax.numpy as jnp
from jax import lax
from jax.experimental import pallas as pl
from jax.experimental.pallas import tpu as pltpu

F32 = jnp.float32
BF16 = jnp.bfloat16

NORM_EPS = 1e-6
LN_EPS = 1e-5
RW_GN_EPS = 64e-5
CHUNK = 128
A_GROUP_DIM = 128
RW_HEAD_DIM = 64
N_GROUPS = 4
EXP_PER_GROUP = 4
N_EXPERTS = N_GROUPS * EXP_PER_GROUP

LANES = 128
SUBLANES = 8
VMEM_LIMIT = 52 * 1024 * 1024


def _cparams(*sem):
    return pltpu.CompilerParams(dimension_semantics=sem, vmem_limit_bytes=VMEM_LIMIT)


def _rms(x, g):
    return x * lax.rsqrt(jnp.mean(x * x, axis=-1, keepdims=True) + NORM_EPS) * g


def _gelu_tanh(x):
    return 0.5 * x * (1.0 + jnp.tanh(0.7978845608028654 * (x + 0.044715 * (x * x * x))))


def _sigmoid(x):
    return 1.0 / (1.0 + jnp.exp(-x))


def _dot(a, b):
    return jnp.dot(a, b, preferred_element_type=F32)


ROW_CHUNK = 256


def _row_chunks(tm):
    ch = min(tm, ROW_CHUNK)
    return [slice(c * ch, (c + 1) * ch) for c in range(tm // ch)]


def _even_inproj_kernel(n_gelu, x_ref, g_ref, w_ref, o_ref, h_ref):
    n = pl.program_id(1)

    def run(first, gelu):
        for rows in _row_chunks(x_ref.shape[0]):
            if first:
                h = _rms(x_ref[rows, :], g_ref[...]).astype(BF16)
                h_ref[rows, :] = h
            else:
                h = h_ref[rows, :]
            acc = _dot(h, w_ref[...])
            o_ref[rows, :] = _gelu_tanh(acc) if gelu else acc

    pl.when(n == 0)(lambda: run(True, n_gelu > 0))
    pl.when((n > 0) & (n < n_gelu))(lambda: run(False, True))
    pl.when((n > 0) & (n >= n_gelu))(lambda: run(False, False))


def _even_inproj(x, g, w, layer, d_a, tm, tn):
    T, D = x.shape
    N = w.shape[2]
    return pl.pallas_call(
        functools.partial(_even_inproj_kernel, (2 * d_a) // tn),
        grid=(T // tm, N // tn),
        in_specs=[pl.BlockSpec((tm, D), lambda i, n: (i, 0)),
                  pl.BlockSpec((1, D), lambda i, n: (0, 0)),
                  pl.BlockSpec((None, D, tn), lambda i, n: (layer, 0, n))],
        out_specs=pl.BlockSpec((tm, tn), lambda i, n: (i, n)),
        out_shape=jax.ShapeDtypeStruct((T, N), F32),
        scratch_shapes=[pltpu.VMEM((tm, D), BF16)],
        compiler_params=_cparams("arbitrary", "arbitrary"),
        name="even_inproj",
    )(x, g, w)


def _even_mix_kernel(prompt, seq_len, tiles_per_seq, tm, d_a,
                     u_ref, v_ref, xi_ref, gb_ref, gc_ref, x_ref, vng_ref, vnb_ref,
                     ws_ref, bias_ref, cw_ref, wout_ref, b1_ref, b2_ref,
                     xo_ref, vo_ref, cxo_ref, y_ref, carry_ref):
    i = pl.program_id(0)
    lc = min(seq_len, CHUNK)

    vg = v_ref[...]
    mu = jnp.mean(vg, axis=-1, keepdims=True)
    vc = vg - mu
    var = jnp.mean(vc * vc, axis=-1, keepdims=True)
    vln = vc * lax.rsqrt(var + LN_EPS) * vng_ref[...] + vnb_ref[...]
    vo_ref[...] = vln

    r = lax.broadcasted_iota(jnp.int32, (CHUNK, CHUNK), 0)
    c = lax.broadcasted_iota(jnp.int32, (CHUNK, CHUNK), 1)
    keep = (r >= c) & ((r // lc) == (c // lc))
    n_groups = d_a // A_GROUP_DIM
    for g in range(n_groups):
        wg = jnp.where(keep, ws_ref[g], 0.0).astype(BF16)
        cols = slice(g * A_GROUP_DIM, (g + 1) * A_GROUP_DIM)
        for ch in range(tm // CHUNK):
            rows = slice(ch * CHUNK, (ch + 1) * CHUNK)
            mixed = _dot(wg, vln[rows, cols].astype(BF16)) + bias_ref[:, cols]
            y_ref[rows, cols] = (u_ref[rows, cols] * mixed).astype(BF16)

    cx = gc_ref[...] * xi_ref[...]
    cxo_ref[...] = cx
    row = lax.broadcasted_iota(jnp.int32, cx.shape, 0)
    r1 = pltpu.roll(cx, 1, 0)
    r2 = pltpu.roll(cx, 2, 0)
    if prompt:
        @pl.when(i % tiles_per_seq == 0)
        def _():
            carry_ref[...] = jnp.zeros_like(carry_ref)
            carry_ref[SUBLANES - 2:SUBLANES, :] = b1_ref[0]
        p1 = carry_ref[SUBLANES - 1:SUBLANES, :]
        p2 = carry_ref[SUBLANES - 2:SUBLANES - 1, :]
        s1 = jnp.where(row == 0, p1, r1)
        s2 = jnp.where(row == 0, p2, jnp.where(row == 1, p1, r2))
        carry_ref[...] = cx[tm - SUBLANES:tm, :]
    else:
        pos = row % seq_len
        s1 = jnp.where(pos == 0, b1_ref[...], r1)
        s2 = jnp.where(pos <= 1, b2_ref[...], r2)
    conv = s2 * cw_ref[0:1, :] + s1 * cw_ref[1:2, :] + cx * cw_ref[2:3, :]
    y_ref[:, d_a:] = (gb_ref[...] * conv).astype(BF16)

    xo_ref[...] = x_ref[...] + _dot(y_ref[...], wout_ref[...])


def _even_mix(proj, x, vn_g, vn_b, ws_eff, bias, conv_w, w_out, layer, b1, b2, prompt, seq_len, tm, d_a):
    T, D = x.shape
    d_b = D - d_a
    tps = max(seq_len // tm, 1)
    col = lambda k: pl.BlockSpec((tm, d_a), lambda i, k=k: (i, k))
    const = lambda shape: pl.BlockSpec(shape, lambda i: (0,) * len(shape))
    if prompt:
        b1_spec = pl.BlockSpec((1, 2, d_b), lambda i: (i // tps, 0, 0))
        b2_spec = pl.BlockSpec((1, 2, d_b), lambda i: (i // tps, 0, 0))
    else:
        b1_spec = pl.BlockSpec((tm, d_b), lambda i: (i, 0))
        b2_spec = pl.BlockSpec((tm, d_b), lambda i: (i, 0))
    return pl.pallas_call(
        functools.partial(_even_mix_kernel, prompt, seq_len, tps, tm, d_a),
        grid=(T // tm,),
        in_specs=[col(0), col(1), col(2), col(3), col(4),
                  pl.BlockSpec((tm, D), lambda i: (i, 0)),
                  const((1, d_a)), const((1, d_a)),
                  const(ws_eff.shape), const(bias.shape), const(conv_w.shape),
                  pl.BlockSpec((None,) + w_out.shape[1:], lambda i: (layer, 0, 0)),
                  b1_spec, b2_spec],
        out_specs=[pl.BlockSpec((tm, D), lambda i: (i, 0)),
                   pl.BlockSpec((tm, d_a), lambda i: (i, 0)),
                   pl.BlockSpec((tm, d_b), lambda i: (i, 0))],
        out_shape=[jax.ShapeDtypeStruct((T, D), F32),
                   jax.ShapeDtypeStruct((T, d_a), F32),
                   jax.ShapeDtypeStruct((T, d_b), F32)],
        scratch_shapes=[pltpu.VMEM((tm, D), BF16), pltpu.VMEM((SUBLANES, d_b), F32)],
        compiler_params=_cparams("arbitrary"),
        name="even_mix",
    )(proj, proj, proj, proj, proj, x, vn_g, vn_b, ws_eff, bias, conv_w, w_out, b1, b2)


def _rwkv_prep_kernel(prompt, seq_len, tiles_per_seq, tm,
                      x_ref, g_ref, mu_ref, b_ref,
                      rkv_ref, xw_ref, xa_ref, xg_ref, h_ref, carry_ref):
    i = pl.program_id(0)
    h = _rms(x_ref[...], g_ref[...])
    h_ref[...] = h
    row = lax.broadcasted_iota(jnp.int32, h.shape, 0)
    r1 = pltpu.roll(h, 1, 0)
    if prompt:
        @pl.when(i % tiles_per_seq == 0)
        def _():
            carry_ref[...] = jnp.zeros_like(carry_ref)
            carry_ref[SUBLANES - 1:SUBLANES, :] = b_ref[0]
        prev = jnp.where(row == 0, carry_ref[SUBLANES - 1:SUBLANES, :], r1)
        carry_ref[...] = h[tm - SUBLANES:tm, :]
    else:
        prev = jnp.where(row % seq_len == 0, b_ref[...], r1)
    xx = prev - h
    rkv_ref[0] = (h + xx * mu_ref[0:1, :]).astype(BF16)
    xw_ref[...] = (h + xx * mu_ref[1:2, :]).astype(BF16)
    rkv_ref[1] = (h + xx * mu_ref[2:3, :]).astype(BF16)
    rkv_ref[2] = (h + xx * mu_ref[3:4, :]).astype(BF16)
    xa_ref[...] = (h + xx * mu_ref[4:5, :]).astype(BF16)
    xg_ref[...] = (h + xx * mu_ref[5:6, :]).astype(BF16)


def _rwkv_prep(x, g, mu, bnd, prompt, seq_len, tm):
    T, D = x.shape
    tps = max(seq_len // tm, 1)
    if prompt:
        b_spec = pl.BlockSpec((1, 1, D), lambda i: (i // tps, 0, 0))
    else:
        b_spec = pl.BlockSpec((tm, D), lambda i: (i, 0))
    tok = pl.BlockSpec((tm, D), lambda i: (i, 0))
    return pl.pallas_call(
        functools.partial(_rwkv_prep_kernel, prompt, seq_len, tps, tm),
        grid=(T // tm,),
        in_specs=[tok, pl.BlockSpec((1, D), lambda i: (0, 0)),
                  pl.BlockSpec(mu.shape, lambda i: (0, 0)), b_spec],
        out_specs=[pl.BlockSpec((3, tm, D), lambda i: (0, i, 0)), tok, tok, tok, tok],
        out_shape=[jax.ShapeDtypeStruct((3, T, D), BF16),
                   jax.ShapeDtypeStruct((T, D), BF16),
                   jax.ShapeDtypeStruct((T, D), BF16),
                   jax.ShapeDtypeStruct((T, D), BF16),
                   jax.ShapeDtypeStruct((T, D), F32)],
        scratch_shapes=[pltpu.VMEM((SUBLANES, D), F32)],
        compiler_params=_cparams("arbitrary"),
        name="rwkv_prep",
    )(x, g, mu, bnd)


HEAD_PAIRS = 16
GROUP_BATCH = 4


def _store_scan_rows(o_ref, lead, acc, nb, lt, row0=0):
    rows = acc.shape[0]
    if nb == 1:
        o_ref[lead + (0, slice(row0, row0 + rows))] = acc.reshape(rows, HEAD_PAIRS, LANES)
    else:
        per_group = GROUP_BATCH * lt
        g0, ng = row0 // per_group, rows // per_group
        a5 = acc.reshape(ng, GROUP_BATCH, lt, HEAD_PAIRS, LANES)
        for bl in range(GROUP_BATCH):
            o_ref[lead + (slice(g0, g0 + ng), slice(None), slice(bl * HEAD_PAIRS, (bl + 1) * HEAD_PAIRS))] = a5[:, bl]


def _load_scan_rows(z_ref, nb, lt):
    if nb == 1:
        return z_ref[0].reshape(lt, HEAD_PAIRS * LANES)
    parts = [z_ref[:, :, bl * HEAD_PAIRS:(bl + 1) * HEAD_PAIRS, :] for bl in range(GROUP_BATCH)]
    return jnp.stack(parts, axis=1).reshape(nb * lt, HEAD_PAIRS * LANES)


def _scan_rows_spec(nb, lt, tps, lead=()):
    nl = len(lead)
    if nb == 1:
        shape = (1,) * nl + (1, lt, HEAD_PAIRS, LANES)
        return shape, lambda i: (i // tps // GROUP_BATCH, i % tps, (i // tps) % GROUP_BATCH, 0)
    shape = (1,) * nl + (nb // GROUP_BATCH, lt, GROUP_BATCH * HEAD_PAIRS, LANES)
    return shape, lambda i: (i, 0, 0, 0)


def _tile_geometry(tm, seq_len):
    lt = min(tm, seq_len)
    return tm // lt, lt, seq_len // lt


def _bmm_kernel(nb, lt, x_ref, w_ref, o_ref):
    _store_scan_rows(o_ref, (0,), _dot(x_ref[0], w_ref[0]), nb, lt)


def _bmm(x, w, layer, tm, B, L):
    n3, T, D = x.shape
    N = w.shape[3]
    nb, lt, tps = _tile_geometry(tm, L)
    shape, idx = _scan_rows_spec(nb, lt, tps, lead=(0,))
    return pl.pallas_call(
        functools.partial(_bmm_kernel, nb, lt),
        grid=(n3, T // tm),
        in_specs=[pl.BlockSpec((1, tm, D), lambda b, i: (b, i, 0)),
                  pl.BlockSpec((None, 1, D, N), lambda b, i: (layer, b, 0, 0))],
        out_specs=pl.BlockSpec(shape, lambda b, i: (b,) + idx(i)),
        out_shape=jax.ShapeDtypeStruct((n3, B // GROUP_BATCH, L, GROUP_BATCH * HEAD_PAIRS, LANES), F32),
        compiler_params=_cparams("arbitrary", "arbitrary"),
        name="rwkv_rkv_proj",
    )(x, w)


def _lora_kernel(mid, out, geom, x_ref, w1_ref, w2_ref, b_ref, o_ref):
    for rows in _row_chunks(x_ref.shape[0]):
        t = _dot(x_ref[rows, :], w1_ref[...])
        if mid == "tanh":
            t = jnp.tanh(t)
        elif mid == "sigmoid":
            t = _sigmoid(t)
        z = _dot(t.astype(BF16), w2_ref[...]) + b_ref[...]
        if out == "decay":
            w_log = jnp.minimum(z, 0.0) - jnp.log(1.0 + jnp.exp(-jnp.abs(z))) - 0.5
            z = jnp.exp(-jnp.exp(w_log))
        elif out == "sigmoid":
            z = _sigmoid(z)
        if geom is None:
            o_ref[rows, :] = z
        else:
            _store_scan_rows(o_ref, (), z, *geom, row0=rows.start)


def _lora(x, w1, w2, b, mid, out, tm, head_pairs=None):
    T, D = x.shape
    R = w1.shape[1]
    if head_pairs is None:
        geom = None
        out_spec = pl.BlockSpec((tm, D), lambda i: (i, 0))
        out_shape = jax.ShapeDtypeStruct((T, D), F32)
    else:
        B, L = head_pairs
        nb, lt, tps = _tile_geometry(tm, L)
        geom = (nb, lt)
        out_spec = pl.BlockSpec(*_scan_rows_spec(nb, lt, tps))
        out_shape = jax.ShapeDtypeStruct((B // GROUP_BATCH, L, GROUP_BATCH * HEAD_PAIRS, LANES), F32)
    return pl.pallas_call(
        functools.partial(_lora_kernel, mid, out, geom),
        grid=(T // tm,),
        in_specs=[pl.BlockSpec((tm, D), lambda i: (i, 0)),
                  pl.BlockSpec((D, R), lambda i: (0, 0)),
                  pl.BlockSpec((R, D), lambda i: (0, 0)),
                  pl.BlockSpec((1, D), lambda i: (0, 0))],
        out_specs=out_spec,
        out_shape=out_shape,
        compiler_params=_cparams("arbitrary"),
        name="rwkv_lora_" + out,
    )(x, w1, w2, b)


def _wkv_scan_kernel(tb, aliased, r_ref, k_ref, v_ref, w_ref, a_ref, s0_ref, kk_c, ka_c, rk_c, gg_c, gb_c, *rest):
    z_ref, st_ref, s_ref, tile_ref, vec_ref, zbuf_ref, slab_ref = rest[1:] if aliased else rest
    n = RW_HEAD_DIM
    half = LANES // 2
    step0 = pl.program_id(1)
    seq_refs = (r_ref, w_ref, k_ref, v_ref, a_ref)
    R, W, K, V, A = range(5)

    nchunk = n * n // LANES
    per_parity = LANES // 2 // GROUP_BATCH
    chain_rows = [pl.ds(b * 2 * per_parity + hl, per_parity, stride=2) for hl in range(2) for b in range(GROUP_BATCH)]

    @pl.when(step0 == 0)
    def _():
        zbuf_ref[...] = jnp.zeros_like(zbuf_ref)
        x2 = s0_ref[...].reshape(LANES, n * n)
        for c in range(nchunk):
            slab_ref[c] = x2[:, c * LANES:(c + 1) * LANES]
        for c in range(nchunk):
            xt = jnp.concatenate([slab_ref[c, rows, :] for rows in chain_rows], axis=0).T
            for il in range(LANES // n):
                s_ref[pl.ds(c * (LANES // n) + il, n, stride=n), :] = xt[il * n:(il + 1) * n]

    def fetch(t, slot):
        for q, ref in enumerate(seq_refs):
            xt = ref[(0,) * (len(ref.shape) - 3) + (t,)].T
            tile_ref[slot, q] = jnp.concatenate([xt[:n], xt[n:]], axis=1)

    def flush(t, slot):
        z = zbuf_ref[slot]
        z_ref[0, t] = jnp.concatenate([z[:, :half], z[:, half:]], axis=0).T

    def recur(slot):
        k_t = tile_ref[slot, K]
        a_t = tile_ref[slot, A]
        v_t = tile_ref[slot, V]
        kk = k_t * kk_c[...]
        nrm = jnp.sqrt(jnp.sum(kk * kk, axis=0, keepdims=True))
        kk = kk / jnp.maximum(nrm, 1e-12)
        kmod = k_t * (1.0 + (a_t - 1.0) * ka_c[...])
        vec_ref[slot, 0] = kk
        vec_ref[slot, 1] = kk * a_t
        vec_ref[slot, 2] = kmod

        sa = jnp.zeros((n, LANES), F32)
        for j in range(n):
            sa = sa + s_ref[j * n:(j + 1) * n, :] * vec_ref[slot, 0, j:j + 1, :]
        y = jnp.zeros((n, LANES), F32)
        for j in range(n):
            s_new = (s_ref[j * n:(j + 1) * n, :] * tile_ref[slot, W, j:j + 1, :]
                     - sa * vec_ref[slot, 1, j:j + 1, :] + v_t * vec_ref[slot, 2, j:j + 1, :])
            s_ref[j * n:(j + 1) * n, :] = s_new
            y = y + s_new * tile_ref[slot, R, j:j + 1, :]

        mu = jnp.mean(y, axis=0, keepdims=True)
        yc = y - mu
        var = jnp.mean(yc * yc, axis=0, keepdims=True)
        yn = yc * lax.rsqrt(var + RW_GN_EPS) * gg_c[...] + gb_c[...]
        bonus = jnp.sum(tile_ref[slot, R] * kmod * rk_c[...], axis=0, keepdims=True) * v_t
        zbuf_ref[slot] = yn + bonus

    fetch(0, 0)

    def pair(u, carry):
        t = 2 * u
        fetch(t + 1, 1)
        flush(jnp.maximum(t - 1, 0), 1)
        recur(0)
        fetch(jnp.minimum(t + 2, tb - 1), 0)
        flush(t, 0)
        recur(1)
        return carry

    lax.fori_loop(0, tb // 2, pair, 0)
    flush(tb - 1, 1)

    @pl.when(step0 == pl.num_programs(1) - 1)
    def _():
        for c in range(nchunk):
            xt = jnp.concatenate([s_ref[pl.ds(c * (LANES // n) + il, n, stride=n), :]
                                  for il in range(LANES // n)], axis=0).T
            for q, rows in enumerate(chain_rows):
                slab_ref[c, rows, :] = xt[q * per_parity:(q + 1) * per_parity]
        x2 = jnp.concatenate([slab_ref[c] for c in range(nchunk)], axis=1)
        st_ref[...] = x2.reshape(st_ref.shape)


def _wkv_scan(rkv, w, a, state, layer, consts, tb, stacked=None):
    _, G, L, half, _ = rkv.shape
    n = RW_HEAD_DIM
    H = state.shape[2]
    assert tb % 2 == 0 and L % tb == 0 and half == LANES // 2 and GROUP_BATCH * H == LANES
    seq3 = lambda m: pl.BlockSpec((1, 1, tb, half, LANES), lambda g, t, m=m: (m, g, t, 0, 0))
    seq = pl.BlockSpec((1, tb, half, LANES), lambda g, t: (g, t, 0, 0))
    st = pl.BlockSpec((None, GROUP_BATCH, H, n, n), lambda g, t: (layer, g, 0, 0, 0))
    cst = pl.BlockSpec((n, LANES), lambda g, t: (0, 0))
    extra, extra_specs, aliases = (), [], {}
    if stacked is not None:
        extra, extra_specs, aliases = (stacked,), [pl.BlockSpec(memory_space=pl.ANY)], {11: 1}
    return pl.pallas_call(
        functools.partial(_wkv_scan_kernel, tb, stacked is not None),
        grid=(G, L // tb),
        in_specs=[seq3(0), seq3(1), seq3(2), seq, seq, st, cst, cst, cst, cst, cst] + extra_specs,
        out_specs=[seq, st],
        out_shape=[jax.ShapeDtypeStruct((G, L, half, LANES), F32),
                   jax.ShapeDtypeStruct(state.shape, F32)],
        scratch_shapes=[pltpu.VMEM((n * n, LANES), F32), pltpu.VMEM((2, 5, n, LANES), F32),
                        pltpu.VMEM((2, 3, n, LANES), F32), pltpu.VMEM((2, n, LANES), F32),
                        pltpu.VMEM((n * n // LANES, LANES, LANES), F32)],
        input_output_aliases=aliases,
        compiler_params=_cparams("arbitrary", "arbitrary"),
        name="rwkv_scan",
    )(rkv, rkv, rkv, w, a, state, *consts, *extra)


def _rwkv_out_kernel(nb, lt, z_ref, g_ref, x_ref, w_ref, o_ref):
    z = _load_scan_rows(z_ref, nb, lt)
    o_ref[...] = x_ref[...] + _dot((z * g_ref[...]).astype(BF16), w_ref[...])


def _rwkv_out(z, g, x, w, layer, tm, L):
    T, D = x.shape
    nb, lt, tps = _tile_geometry(tm, L)
    tok = pl.BlockSpec((tm, D), lambda i: (i, 0))
    return pl.pallas_call(
        functools.partial(_rwkv_out_kernel, nb, lt),
        grid=(T // tm,),
        in_specs=[pl.BlockSpec(*_scan_rows_spec(nb, lt, tps)),
                  tok, tok, pl.BlockSpec((None, D, D), lambda i: (layer, 0, 0))],
        out_specs=tok,
        out_shape=jax.ShapeDtypeStruct((T, D), F32),
        compiler_params=_cparams("arbitrary"),
        name="rwkv_out",
    )(z, g, x, w)


def _moe_route_kernel(x_ref, g_ref, whi_ref, wlo_ref, b_ref, info_ref, cnt_ref, run_ref):
    h = _rms(x_ref[...], g_ref[...])
    h_hi = h.astype(BF16)
    h_lo = (h - h_hi.astype(F32)).astype(BF16)
    nt = (((1,), (1,)), ((), ()))
    logits = (lax.dot_general(whi_ref[...], h_hi, nt, preferred_element_type=F32)
              + lax.dot_general(whi_ref[...], h_lo, nt, preferred_element_type=F32)
              + lax.dot_general(wlo_ref[...], h_hi, nt, preferred_element_type=F32)
              + b_ref[...])
    gl = [logits[k:k + 1, :] for k in range(N_GROUPS)]
    gmax = functools.reduce(jnp.maximum, gl)
    sel, taken = [], jnp.zeros_like(gmax)
    for k in range(N_GROUPS):
        s = jnp.where((gl[k] == gmax) & (taken == 0.0), 1.0, 0.0)
        taken = taken + s
        sel.append(s)
    p_group = 1.0 / functools.reduce(jnp.add, [jnp.exp(x - gmax) for x in gl])

    el = []
    for m in range(EXP_PER_GROUP):
        rows = [logits[SUBLANES + k * EXP_PER_GROUP + m:SUBLANES + k * EXP_PER_GROUP + m + 1, :]
                for k in range(N_GROUPS)]
        el.append(functools.reduce(jnp.add, [jnp.where(sel[k] > 0.0, rows[k], 0.0) for k in range(N_GROUPS)]))
    emax = functools.reduce(jnp.maximum, el)
    ee = [jnp.exp(x - emax) for x in el]
    esum = functools.reduce(jnp.add, ee)
    prob = [x / esum for x in ee]

    def first_argmax(vals):
        vmax = functools.reduce(jnp.maximum, vals)
        hot, used = [], jnp.zeros_like(vmax)
        for x in vals:
            s = jnp.where((x == vmax) & (used == 0.0), 1.0, 0.0)
            used = used + s
            hot.append(s)
        return hot, vmax

    t1, p1 = first_argmax(prob)
    rest = [jnp.where(t1[m] > 0.0, -1.0, prob[m]) for m in range(EXP_PER_GROUP)]
    t2, p2 = first_argmax(rest)
    scale = p_group / (p1 + p2)
    group = functools.reduce(jnp.add, [sel[k] * float(k * EXP_PER_GROUP) for k in range(N_GROUPS)])
    e1 = group + functools.reduce(jnp.add, [t1[m] * float(m) for m in range(EXP_PER_GROUP)])
    e2 = group + functools.reduce(jnp.add, [t2[m] * float(m) for m in range(EXP_PER_GROUP)])

    @pl.when(pl.program_id(0) == 0)
    def _():
        run_ref[...] = jnp.zeros_like(run_ref)

    tm = e1.shape[1]
    hot1 = jnp.concatenate([sel[k] * t1[m] for k in range(N_GROUPS) for m in range(EXP_PER_GROUP)], axis=0)
    hot2 = jnp.concatenate([sel[k] * t2[m] for k in range(N_GROUPS) for m in range(EXP_PER_GROUP)], axis=0)
    earlier = (lax.broadcasted_iota(jnp.int32, (tm, tm), 0) < lax.broadcasted_iota(jnp.int32, (tm, tm), 1))
    earlier = jnp.where(earlier, 1.0, 0.0).astype(BF16)
    pre1 = _dot(hot1.astype(BF16), earlier)
    pre2 = _dot(hot2.astype(BF16), earlier)
    tot1 = jnp.sum(hot1, axis=1, keepdims=True)
    tot2 = jnp.sum(hot2, axis=1, keepdims=True)
    base = run_ref[:, 0:1]
    rank1 = jnp.sum(hot1 * (base + pre1), axis=0, keepdims=True)
    rank2 = jnp.sum(hot2 * (base + tot1 + pre2), axis=0, keepdims=True)
    run_ref[...] = run_ref[...] + (tot1 + tot2)
    cnt_ref[...] = run_ref[...]

    info_ref[...] = jnp.zeros_like(info_ref)
    info_ref[0:1, :] = e1
    info_ref[1:2, :] = e2
    info_ref[2:3, :] = p1 * scale
    info_ref[3:4, :] = p2 * scale
    info_ref[4:5, :] = rank1
    info_ref[5:6, :] = rank2


def _moe_route(x, g, w_hi, w_lo, b, tm):
    T, D = x.shape
    R = w_hi.shape[0]
    return pl.pallas_call(
        _moe_route_kernel,
        grid=(T // tm,),
        in_specs=[pl.BlockSpec((tm, D), lambda i: (i, 0)),
                  pl.BlockSpec((1, D), lambda i: (0, 0)),
                  pl.BlockSpec((R, D), lambda i: (0, 0)),
                  pl.BlockSpec((R, D), lambda i: (0, 0)),
                  pl.BlockSpec((R, 1), lambda i: (0, 0))],
        out_specs=[pl.BlockSpec((SUBLANES, tm), lambda i: (0, i)),
                   pl.BlockSpec((N_EXPERTS, LANES), lambda i: (0, 0))],
        out_shape=[jax.ShapeDtypeStruct((SUBLANES, T), F32),
                   jax.ShapeDtypeStruct((N_EXPERTS, LANES), F32)],
        scratch_shapes=[pltpu.VMEM((N_EXPERTS, LANES), F32)],
        compiler_params=_cparams("arbitrary"),
        name="moe_route",
    )(x, g, w_hi, w_lo, b)


def _moe_plan(info, counts, tr):
    T = info.shape[1]
    P = 2 * T
    n_tiles = P // tr + N_EXPERTS
    experts = jnp.arange(N_EXPERTS, dtype=jnp.int32)
    eid = info[0:2].astype(jnp.int32).reshape(P)
    rank = info[4:6].astype(jnp.int32).reshape(P)
    tok = jnp.tile(jnp.arange(T, dtype=jnp.int32), 2)
    counts = counts[:, 0].astype(jnp.int32)
    padded = ((counts + tr - 1) // tr) * tr
    ends = jnp.cumsum(padded)
    starts = ends - padded
    slot = jnp.sum(jnp.where(eid[:, None] == experts[None, :], starts[None, :], 0), axis=1) + rank
    src_tok = jnp.zeros((n_tiles * tr,), jnp.int32).at[slot].set(tok)
    n_used = ends[-1] // tr
    tile_row = jnp.minimum(jnp.arange(n_tiles, dtype=jnp.int32), n_used - 1) * tr
    tile_expert = jnp.sum((tile_row[:, None] >= ends[None, :]).astype(jnp.int32), axis=1)
    return src_tok, tile_expert, n_used.reshape(1), slot


def _moe_expert_kernel(tr, src_ref, texp_ref, nused_ref, x_hbm, g_ref, wg_ref, wu_ref, wd_ref, o_ref,
                       xbuf, wg_b, wu_b, wd_b, sem):
    i = pl.program_id(0)
    n_used = nused_ref[0]

    def start_gather(tile, buf):
        def body(r, c):
            tok = src_ref[tile * tr + r]
            pltpu.make_async_copy(x_hbm.at[pl.ds(tok, 1)], xbuf.at[buf, pl.ds(r, 1)], sem.at[buf]).start()
            return c
        lax.fori_loop(0, tr, body, 0, unroll=8)

    def wait_gather(buf):
        pltpu.make_async_copy(x_hbm.at[pl.ds(0, tr)], xbuf.at[buf], sem.at[buf]).wait()

    @pl.when(i == 0)
    def _():
        start_gather(0, 0)

    @pl.when(i + 1 < n_used)
    def _():
        start_gather(i + 1, (i + 1) % 2)

    @pl.when((i == 0) | (texp_ref[i] != texp_ref[jnp.maximum(i - 1, 0)]))
    def _():
        wg_b[...] = wg_ref[0].astype(BF16)
        wu_b[...] = wu_ref[0].astype(BF16)
        wd_b[...] = wd_ref[0].astype(BF16)

    @pl.when(i < n_used)
    def _():
        buf = i % 2
        wait_gather(buf)
        h = _rms(xbuf[buf], g_ref[...]).astype(BF16)
        hg = _dot(h, wg_b[...])
        hu = _dot(h, wu_b[...])
        act = hg * _sigmoid(hg) * hu
        o_ref[...] = _dot(act.astype(BF16), wd_b[...])

    @pl.when(i >= n_used)
    def _():
        o_ref[...] = jnp.zeros_like(o_ref)


def _moe_experts(x, g, src_tok, tile_expert, n_used, w_gate, w_up, w_down, layer, tr):
    T, D = x.shape
    Fd = w_gate.shape[3]
    n_tiles = src_tok.shape[0] // tr
    grid_spec = pltpu.PrefetchScalarGridSpec(
        num_scalar_prefetch=3,
        grid=(n_tiles,),
        in_specs=[pl.BlockSpec(memory_space=pl.ANY),
                  pl.BlockSpec((1, D), lambda i, s, e, n: (0, 0)),
                  pl.BlockSpec((None, 1, D, Fd), lambda i, s, e, n: (layer, e[i], 0, 0)),
                  pl.BlockSpec((None, 1, D, Fd), lambda i, s, e, n: (layer, e[i], 0, 0)),
                  pl.BlockSpec((None, 1, Fd, D), lambda i, s, e, n: (layer, e[i], 0, 0))],
        out_specs=pl.BlockSpec((tr, D), lambda i, s, e, n: (i, 0)),
        scratch_shapes=[pltpu.VMEM((2, tr, D), F32), pltpu.VMEM((D, Fd), BF16), pltpu.VMEM((D, Fd), BF16),
                        pltpu.VMEM((Fd, D), BF16), pltpu.SemaphoreType.DMA((2,))],
    )
    return pl.pallas_call(
        functools.partial(_moe_expert_kernel, tr),
        grid_spec=grid_spec,
        out_shape=jax.ShapeDtypeStruct((n_tiles * tr, D), F32),
        compiler_params=_cparams("arbitrary"),
        name="moe_experts",
    )(src_tok, tile_expert, n_used, x, g, w_gate, w_up, w_down)


def _moe_combine_kernel(tm, slot_ref, info_ref, x_ref, ys_hbm, o_ref, ybuf, sem):
    i = pl.program_id(0)
    n = pl.num_programs(0)
    T = n * tm

    def start_gather(tile, buf):
        def body(r, c):
            for k in range(2):
                s = slot_ref[k * T + tile * tm + r]
                pltpu.make_async_copy(ys_hbm.at[pl.ds(s, 1)], ybuf.at[buf, k, pl.ds(r, 1)], sem.at[buf]).start()
            return c
        lax.fori_loop(0, tm, body, 0, unroll=4)

    def wait_gather(buf):
        for k in range(2):
            pltpu.make_async_copy(ys_hbm.at[pl.ds(0, tm)], ybuf.at[buf, k], sem.at[buf]).wait()

    @pl.when(i == 0)
    def _():
        start_gather(0, 0)

    @pl.when(i + 1 < n)
    def _():
        start_gather(i + 1, (i + 1) % 2)

    buf = i % 2
    wait_gather(buf)
    gates = info_ref[...].T
    o_ref[...] = x_ref[...] + gates[:, 2:3] * ybuf[buf, 0] + gates[:, 3:4] * ybuf[buf, 1]


def _moe_combine(x, info, slot, ys, tm):
    T, D = x.shape
    grid_spec = pltpu.PrefetchScalarGridSpec(
        num_scalar_prefetch=1,
        grid=(T // tm,),
        in_specs=[pl.BlockSpec((SUBLANES, tm), lambda i, s: (0, i)),
                  pl.BlockSpec((tm, D), lambda i, s: (i, 0)),
                  pl.BlockSpec(memory_space=pl.ANY)],
        out_specs=pl.BlockSpec((tm, D), lambda i, s: (i, 0)),
        scratch_shapes=[pltpu.VMEM((2, 2, tm, D), F32), pltpu.SemaphoreType.DMA((2,))],
    )
    return pl.pallas_call(
        functools.partial(_moe_combine_kernel, tm),
        grid_spec=grid_spec,
        out_shape=jax.ShapeDtypeStruct((T, D), F32),
        compiler_params=_cparams("arbitrary"),
        name="moe_combine",
    )(slot, info, x, ys)


def _ple_kernel(x_ref, p_ref, g_ref, wg_ref, wp_ref, o_ref):
    for rows in _row_chunks(x_ref.shape[0]):
        x = x_ref[rows, :]
        gate = _sigmoid(_dot(_rms(x, g_ref[...]).astype(BF16), wg_ref[...]))
        o_ref[rows, :] = x + gate * _dot(p_ref[rows, :].astype(BF16), wp_ref[...])


def _ple(x, p, g, w_gate, w_proj, layer, tm):
    T, D = x.shape
    P = p.shape[2]
    return pl.pallas_call(
        _ple_kernel,
        grid=(T // tm,),
        in_specs=[pl.BlockSpec((tm, D), lambda i: (i, 0)),
                  pl.BlockSpec((None, tm, P), lambda i: (layer, i, 0)),
                  pl.BlockSpec((1, D), lambda i: (0, 0)),
                  pl.BlockSpec((None, D, D), lambda i: (layer, 0, 0)),
                  pl.BlockSpec((None, P, D), lambda i: (layer, 0, 0))],
        out_specs=pl.BlockSpec((tm, D), lambda i: (i, 0)),
        out_shape=jax.ShapeDtypeStruct((T, D), F32),
        compiler_params=_cparams("arbitrary"),
        name="ple",
    )(x, p, g, w_gate, w_proj)


def _final_norm_kernel(x_ref, g_ref, o_ref):
    o_ref[...] = _rms(x_ref[...], g_ref[...])


def _final_norm(x, g, tm):
    T, D = x.shape
    return pl.pallas_call(
        _final_norm_kernel,
        grid=(T // tm,),
        in_specs=[pl.BlockSpec((tm, D), lambda i: (i, 0)), pl.BlockSpec((1, D), lambda i: (0, 0))],
        out_specs=pl.BlockSpec((tm, D), lambda i: (i, 0)),
        out_shape=jax.ShapeDtypeStruct((T, D), F32),
        compiler_params=_cparams("arbitrary"),
        name="final_norm",
    )(x, g)


def _chain_tile(p, H):
    t = p.reshape(H // 2, 2, RW_HEAD_DIM).transpose(2, 1, 0)
    bg = LANES // H
    return jnp.broadcast_to(t[:, :, None, :], (RW_HEAD_DIM, 2, bg, H // 2)).reshape(RW_HEAD_DIM, LANES)


def _trunk(x, p, conv0, shift0, wkv0, W, prompt):
    B, L, D = x.shape
    depth = p.shape[0]
    T = B * L
    d_a = W["vn_g"].shape[1]
    H = D // RW_HEAD_DIM
    lc = min(L, CHUNK)
    x = x.reshape(T, D)
    p = p.reshape(depth, T, -1)
    tm = 512
    tm_mix = 256
    tr = 256 if T >= 4096 else 128
    chunk_v, conv_new, shift_new, wkv_stack = [], [], [], None
    for i in range(depth):
        j = i // 2
        if i % 2 == 0:
            proj = _even_inproj(x, W["g_mix"][i][None], W["w_in"], j, d_a, min(T, 2 * tm), d_a)
            ws_eff = jnp.tile(W["w_s"][j][:, :lc, :lc], (1, CHUNK // lc, CHUNK // lc))
            bias = jnp.tile(jnp.repeat(W["b_s"][j][:, :lc].T, A_GROUP_DIM, axis=1), (CHUNK // lc, 1))
            st = conv0[j]
            if prompt:
                b1 = b2 = st
            else:
                zero = jnp.zeros((B, L, st.shape[-1]), F32)
                b1 = zero.at[:, 0].set(st[:, 1]).reshape(T, -1)
                b2 = zero.at[:, 0].set(st[:, 0]).at[:, 1].set(st[:, 1]).reshape(T, -1)
            x, v_all, cx_all = _even_mix(proj, x, W["vn_g"][j][None], W["vn_b"][j][None], ws_eff, bias,
                                         W["conv_w"][j], W["w_out"], j, b1, b2, prompt, L, tm_mix, d_a)
            start = ((L - 1) // CHUNK) * CHUNK
            chunk_v.append(v_all.reshape(B, L, -1)[:, start:])
            conv_new.append(cx_all.reshape(B, L, -1)[:, L - 2:])
        else:
            st = shift0[j]
            if prompt:
                bnd = st[:, None, :]
            else:
                bnd = jnp.zeros((B, L, D), F32).at[:, 0].set(st).reshape(T, D)
            xrkv, xw, xa, xg, h_all = _rwkv_prep(x, W["g_mix"][i][None], W["rw_mu"][j], bnd, prompt, L, tm_mix)
            rkv = _bmm(xrkv, W["rw_w_rkv"], j, tm, B, L)
            decay = _lora(xw, W["rw_w1"][j], W["rw_w2"][j], W["rw_w0"][j][None], "tanh", "decay", tm, (B, L))
            a = _lora(xa, W["rw_a1"][j], W["rw_a2"][j], W["rw_a0"][j][None], "none", "sigmoid", tm, (B, L))
            g = _lora(xg, W["rw_g1"][j], W["rw_g2"][j], jnp.zeros((1, D), F32), "sigmoid", "none", tm)
            consts = (_chain_tile(W["rw_k_k"][j], H), _chain_tile(W["rw_k_a"][j], H),
                      _chain_tile(W["rw_r_k"][j].reshape(-1), H),
                      _chain_tile(W["rw_gn_g"][j], H), _chain_tile(W["rw_gn_b"][j], H))
            z, wkv_stack = _wkv_scan(rkv, decay, a, wkv0, j, consts, min(L, 32), wkv_stack)
            x = _rwkv_out(z, g, x, W["rw_w_o"], j, tm_mix, L)
            shift_new.append(h_all.reshape(B, L, D)[:, -1])
        info, counts = _moe_route(x, W["g_ffn"][i][None], W["moe_r_hi"][i], W["moe_r_lo"][i], W["moe_r_b"][i], tm)
        src_tok, tile_expert, n_used, slot = _moe_plan(info, counts, tr)
        ys = _moe_experts(x, W["g_ffn"][i][None], src_tok, tile_expert, n_used,
                          W["moe_w_gate"], W["moe_w_up"], W["moe_w_down"], i, tr)
        x = _moe_combine(x, info, slot, ys, tm_mix)
        x = _ple(x, p, W["ple_g"][i][None], W["ple_w_gate"], W["ple_w_proj"], i, tm)
    y = _final_norm(x, W["g_final"][None], tm).reshape(B, L, D)
    return y, jnp.stack(chunk_v), jnp.stack(conv_new), jnp.stack(shift_new), wkv_stack


def _pad_rank(w1, w2):
    r = w1.shape[-1]
    rp = -(-r // LANES) * LANES
    w1 = jnp.pad(w1, ((0, 0), (0, 0), (0, rp - r)))
    w2 = jnp.pad(w2, ((0, 0), (0, rp - r), (0, 0)))
    return w1.astype(BF16), w2.astype(BF16)


def kernel(x_prompt, x_sample, state_conv, state_shift, state_wkv, p_prompt, p_sample, g_mix, g_ffn, g_final, w_in_even, vn_g, vn_b, w_s, b_s, conv_w, w_out_even, rw_mu, rw_w_rkv, rw_w0, rw_w1, rw_w2, rw_a0, rw_a1, rw_a2, rw_g1, rw_g2, rw_k_k, rw_k_a, rw_r_k, rw_gn_g, rw_gn_b, rw_w_o, moe_w_gr, moe_b_gr, moe_w_er, moe_b_er, moe_w_gate, moe_w_up, moe_w_down, ple_g, ple_w_gate, ple_w_proj):
    depth, D = g_mix.shape
    r_w = jnp.zeros((depth, 32, D), F32)
    r_w = r_w.at[:, :N_GROUPS].set(jnp.swapaxes(moe_w_gr, 1, 2))
    r_w = r_w.at[:, SUBLANES:SUBLANES + N_EXPERTS].set(jnp.swapaxes(moe_w_er, 1, 2))
    r_hi = r_w.astype(BF16)
    r_lo = (r_w - r_hi.astype(F32)).astype(BF16)
    r_b = jnp.zeros((depth, 32, 1), F32)
    r_b = r_b.at[:, :N_GROUPS, 0].set(moe_b_gr).at[:, SUBLANES:SUBLANES + N_EXPERTS, 0].set(moe_b_er)
    w1, w2 = _pad_rank(rw_w1, rw_w2)
    a1, a2 = _pad_rank(rw_a1, rw_a2)
    g1, g2 = _pad_rank(rw_g1, rw_g2)
    W = dict(g_mix=g_mix, g_ffn=g_ffn, g_final=g_final,
             w_in=w_in_even.astype(BF16), vn_g=vn_g, vn_b=vn_b, w_s=w_s, b_s=b_s, conv_w=conv_w,
             w_out=w_out_even.astype(BF16),
             rw_mu=rw_mu, rw_w_rkv=rw_w_rkv.astype(BF16), rw_w0=rw_w0, rw_w1=w1, rw_w2=w2,
             rw_a0=rw_a0, rw_a1=a1, rw_a2=a2, rw_g1=g1, rw_g2=g2,
             rw_k_k=rw_k_k, rw_k_a=rw_k_a, rw_r_k=rw_r_k, rw_gn_g=rw_gn_g, rw_gn_b=rw_gn_b,
             rw_w_o=rw_w_o.astype(BF16),
             moe_r_hi=r_hi, moe_r_lo=r_lo, moe_r_b=r_b,
             moe_w_gate=moe_w_gate, moe_w_up=moe_w_up, moe_w_down=moe_w_down,
             ple_g=ple_g, ple_w_gate=ple_w_gate.astype(BF16), ple_w_proj=ple_w_proj.astype(BF16))
    bp = x_prompt.shape[0]
    n_even, n_odd = state_conv.shape[0], state_shift.shape[0]
    conv0 = jnp.zeros((n_even, bp) + state_conv.shape[2:], F32)
    shift0 = jnp.zeros((n_odd, bp, D), F32)
    wkv0 = jnp.zeros((n_odd, bp) + state_wkv.shape[2:], F32)
    y_p, cv_p, conv_p, shift_p, wkv_p = _trunk(x_prompt, p_prompt, conv0, shift0, wkv0, W, True)
    y_s, cv_s, conv_s, shift_s, wkv_s = _trunk(x_sample, p_sample, state_conv, state_shift, state_wkv, W, False)
    return (y_p, y_s, cv_p, conv_p, shift_p, wkv_p, cv_s, conv_s, shift_s, wkv_s)
```

```python
import functools

import jax
import jax.numpy as jnp
from jax import lax
from jax.experimental import pallas as pl
from jax.experimental.pallas import tpu as pltpu

F32 = jnp.float32
BF16 = jnp.bfloat16

NORM_EPS = 1e-6
LN_EPS = 1e-5
RW_GN_EPS = 64e-5
CHUNK = 128
A_GROUP_DIM = 128
RW_HEAD_DIM = 64
N_GROUPS = 4
EXP_PER_GROUP = 4
N_EXPERTS = N_GROUPS * EXP_PER_GROUP

LANES = 128
SUBLANES = 8
VMEM_LIMIT = 52 * 1024 * 1024


def _cparams(*sem):
    return pltpu.CompilerParams(dimension_semantics=sem, vmem_limit_bytes=VMEM_LIMIT)


def _rms(x, g):
    return x * lax.rsqrt(jnp.mean(x * x, axis=-1, keepdims=True) + NORM_EPS) * g


def _gelu_tanh(x):
    return 0.5 * x * (1.0 + jnp.tanh(0.7978845608028654 * (x + 0.044715 * (x * x * x))))


def _sigmoid(x):
    return 1.0 / (1.0 + jnp.exp(-x))


def _dot(a, b):
    return jnp.dot(a, b, preferred_element_type=F32)


ROW_CHUNK = 256


def _row_chunks(tm):
    ch = min(tm, ROW_CHUNK)
    return [slice(c * ch, (c + 1) * ch) for c in range(tm // ch)]


def _even_inproj_kernel(n_gelu, x_ref, g_ref, w_ref, o_ref, h_ref):
    n = pl.program_id(1)

    def run(first, gelu):
        for rows in _row_chunks(x_ref.shape[0]):
            if first:
                h = _rms(x_ref[rows, :], g_ref[...]).astype(BF16)
                h_ref[rows, :] = h
            else:
                h = h_ref[rows, :]
            acc = _dot(h, w_ref[...])
            o_ref[rows, :] = _gelu_tanh(acc) if gelu else acc

    pl.when(n == 0)(lambda: run(True, n_gelu > 0))
    pl.when((n > 0) & (n < n_gelu))(lambda: run(False, True))
    pl.when((n > 0) & (n >= n_gelu))(lambda: run(False, False))


def _even_inproj(x, g, w, layer, d_a, tm, tn):
    T, D = x.shape
    N = w.shape[2]
    return pl.pallas_call(
        functools.partial(_even_inproj_kernel, (2 * d_a) // tn),
        grid=(T // tm, N // tn),
        in_specs=[pl.BlockSpec((tm, D), lambda i, n: (i, 0)),
                  pl.BlockSpec((1, D), lambda i, n: (0, 0)),
                  pl.BlockSpec((None, D, tn), lambda i, n: (layer, 0, n))],
        out_specs=pl.BlockSpec((tm, tn), lambda i, n: (i, n)),
        out_shape=jax.ShapeDtypeStruct((T, N), F32),
        scratch_shapes=[pltpu.VMEM((tm, D), BF16)],
        compiler_params=_cparams("arbitrary", "arbitrary"),
        name="even_inproj",
    )(x, g, w)


def _even_mix_kernel(prompt, seq_len, tiles_per_seq, tm, d_a,
                     u_ref, v_ref, xi_ref, gb_ref, gc_ref, x_ref, vng_ref, vnb_ref,
                     ws_ref, bias_ref, cw_ref, wout_ref, b1_ref, b2_ref,
                     xo_ref, vo_ref, cxo_ref, y_ref, carry_ref):
    i = pl.program_id(0)
    lc = min(seq_len, CHUNK)

    vg = v_ref[...]
    mu = jnp.mean(vg, axis=-1, keepdims=True)
    vc = vg - mu
    var = jnp.mean(vc * vc, axis=-1, keepdims=True)
    vln = vc * lax.rsqrt(var + LN_EPS) * vng_ref[...] + vnb_ref[...]
    vo_ref[...] = vln

    r = lax.broadcasted_iota(jnp.int32, (CHUNK, CHUNK), 0)
    c = lax.broadcasted_iota(jnp.int32, (CHUNK, CHUNK), 1)
    keep = (r >= c) & ((r // lc) == (c // lc))
    n_groups = d_a // A_GROUP_DIM
    for g in range(n_groups):
        wg = jnp.where(keep, ws_ref[g], 0.0).astype(BF16)
        cols = slice(g * A_GROUP_DIM, (g + 1) * A_GROUP_DIM)
        for ch in range(tm // CHUNK):
            rows = slice(ch * CHUNK, (ch + 1) * CHUNK)
            mixed = _dot(wg, vln[rows, cols].astype(BF16)) + bias_ref[:, cols]
            y_ref[rows, cols] = (u_ref[rows, cols] * mixed).astype(BF16)

    cx = gc_ref[...] * xi_ref[...]
    cxo_ref[...] = cx
    row = lax.broadcasted_iota(jnp.int32, cx.shape, 0)
    r1 = pltpu.roll(cx, 1, 0)
    r2 = pltpu.roll(cx, 2, 0)
    if prompt:
        @pl.when(i % tiles_per_seq == 0)
        def _():
            carry_ref[...] = jnp.zeros_like(carry_ref)
            carry_ref[SUBLANES - 2:SUBLANES, :] = b1_ref[0]
        p1 = carry_ref[SUBLANES - 1:SUBLANES, :]
        p2 = carry_ref[SUBLANES - 2:SUBLANES - 1, :]
        s1 = jnp.where(row == 0, p1, r1)
        s2 = jnp.where(row == 0, p2, jnp.where(row == 1, p1, r2))
        carry_ref[...] = cx[tm - SUBLANES:tm, :]
    else:
        pos = row % seq_len
        s1 = jnp.where(pos == 0, b1_ref[...], r1)
        s2 = jnp.where(pos <= 1, b2_ref[...], r2)
    conv = s2 * cw_ref[0:1, :] + s1 * cw_ref[1:2, :] + cx * cw_ref[2:3, :]
    y_ref[:, d_a:] = (gb_ref[...] * conv).astype(BF16)

    xo_ref[...] = x_ref[...] + _dot(y_ref[...], wout_ref[...])


def _even_mix(proj, x, vn_g, vn_b, ws_eff, bias, conv_w, w_out, layer, b1, b2, prompt, seq_len, tm, d_a):
    T, D = x.shape
    d_b = D - d_a
    tps = max(seq_len // tm, 1)
    col = lambda k: pl.BlockSpec((tm, d_a), lambda i, k=k: (i, k))
    const = lambda shape: pl.BlockSpec(shape, lambda i: (0,) * len(shape))
    if prompt:
        b1_spec = pl.BlockSpec((1, 2, d_b), lambda i: (i // tps, 0, 0))
        b2_spec = pl.BlockSpec((1, 2, d_b), lambda i: (i // tps, 0, 0))
    else:
        b1_spec = pl.BlockSpec((tm, d_b), lambda i: (i, 0))
        b2_spec = pl.BlockSpec((tm, d_b), lambda i: (i, 0))
    return pl.pallas_call(
        functools.partial(_even_mix_kernel, prompt, seq_len, tps, tm, d_a),
        grid=(T // tm,),
        in_specs=[col(0), col(1), col(2), col(3), col(4),
                  pl.BlockSpec((tm, D), lambda i: (i, 0)),
                  const((1, d_a)), const((1, d_a)),
                  const(ws_eff.shape), const(bias.shape), const(conv_w.shape),
                  pl.BlockSpec((None,) + w_out.shape[1:], lambda i: (layer, 0, 0)),
                  b1_spec, b2_spec],
        out_specs=[pl.BlockSpec((tm, D), lambda i: (i, 0)),
                   pl.BlockSpec((tm, d_a), lambda i: (i, 0)),
                   pl.BlockSpec((tm, d_b), lambda i: (i, 0))],
        out_shape=[jax.ShapeDtypeStruct((T, D), F32),
                   jax.ShapeDtypeStruct((T, d_a), F32),
                   jax.ShapeDtypeStruct((T, d_b), F32)],
        scratch_shapes=[pltpu.VMEM((tm, D), BF16), pltpu.VMEM((SUBLANES, d_b), F32)],
        compiler_params=_cparams("arbitrary"),
        name="even_mix",
    )(proj, proj, proj, proj, proj, x, vn_g, vn_b, ws_eff, bias, conv_w, w_out, b1, b2)


def _rwkv_prep_kernel(prompt, seq_len, tiles_per_seq, tm,
                      x_ref, g_ref, mu_ref, b_ref,
                      rkv_ref, xw_ref, xa_ref, xg_ref, h_ref, carry_ref):
    i = pl.program_id(0)
    h = _rms(x_ref[...], g_ref[...])
    h_ref[...] = h
    row = lax.broadcasted_iota(jnp.int32, h.shape, 0)
    r1 = pltpu.roll(h, 1, 0)
    if prompt:
        @pl.when(i % tiles_per_seq == 0)
        def _():
            carry_ref[...] = jnp.zeros_like(carry_ref)
            carry_ref[SUBLANES - 1:SUBLANES, :] = b_ref[0]
        prev = jnp.where(row == 0, carry_ref[SUBLANES - 1:SUBLANES, :], r1)
        carry_ref[...] = h[tm - SUBLANES:tm, :]
    else:
        prev = jnp.where(row % seq_len == 0, b_ref[...], r1)
    xx = prev - h
    rkv_ref[0] = (h + xx * mu_ref[0:1, :]).astype(BF16)
    xw_ref[...] = (h + xx * mu_ref[1:2, :]).astype(BF16)
    rkv_ref[1] = (h + xx * mu_ref[2:3, :]).astype(BF16)
    rkv_ref[2] = (h + xx * mu_ref[3:4, :]).astype(BF16)
    xa_ref[...] = (h + xx * mu_ref[4:5, :]).astype(BF16)
    xg_ref[...] = (h + xx * mu_ref[5:6, :]).astype(BF16)


def _rwkv_prep(x, g, mu, bnd, prompt, seq_len, tm):
    T, D = x.shape
    tps = max(seq_len // tm, 1)
    if prompt:
        b_spec = pl.BlockSpec((1, 1, D), lambda i: (i // tps, 0, 0))
    else:
        b_spec = pl.BlockSpec((tm, D), lambda i: (i, 0))
    tok = pl.BlockSpec((tm, D), lambda i: (i, 0))
    return pl.pallas_call(
        functools.partial(_rwkv_prep_kernel, prompt, seq_len, tps, tm),
        grid=(T // tm,),
        in_specs=[tok, pl.BlockSpec((1, D), lambda i: (0, 0)),
                  pl.BlockSpec(mu.shape, lambda i: (0, 0)), b_spec],
        out_specs=[pl.BlockSpec((3, tm, D), lambda i: (0, i, 0)), tok, tok, tok, tok],
        out_shape=[jax.ShapeDtypeStruct((3, T, D), BF16),
                   jax.ShapeDtypeStruct((T, D), BF16),
                   jax.ShapeDtypeStruct((T, D), BF16),
                   jax.ShapeDtypeStruct((T, D), BF16),
                   jax.ShapeDtypeStruct((T, D), F32)],
        scratch_shapes=[pltpu.VMEM((SUBLANES, D), F32)],
        compiler_params=_cparams("arbitrary"),
        name="rwkv_prep",
    )(x, g, mu, bnd)


HEAD_PAIRS = 16
GROUP_BATCH = 4


def _store_scan_rows(o_ref, lead, acc, nb, lt, row0=0):
    rows = acc.shape[0]
    if nb == 1:
        o_ref[lead + (0, slice(row0, row0 + rows))] = acc.reshape(rows, HEAD_PAIRS, LANES)
    else:
        per_group = GROUP_BATCH * lt
        g0, ng = row0 // per_group, rows // per_group
        a5 = acc.reshape(ng, GROUP_BATCH, lt, HEAD_PAIRS, LANES)
        for bl in range(GROUP_BATCH):
            o_ref[lead + (slice(g0, g0 + ng), slice(None), slice(bl * HEAD_PAIRS, (bl + 1) * HEAD_PAIRS))] = a5[:, bl]


def _load_scan_rows(z_ref, nb, lt):
    if nb == 1:
        return z_ref[0].reshape(lt, HEAD_PAIRS * LANES)
    parts = [z_ref[:, :, bl * HEAD_PAIRS:(bl + 1) * HEAD_PAIRS, :] for bl in range(GROUP_BATCH)]
    return jnp.stack(parts, axis=1).reshape(nb * lt, HEAD_PAIRS * LANES)


def _scan_rows_spec(nb, lt, tps, lead=()):
    nl = len(lead)
    if nb == 1:
        shape = (1,) * nl + (1, lt, HEAD_PAIRS, LANES)
        return shape, lambda i: (i // tps // GROUP_BATCH, i % tps, (i // tps) % GROUP_BATCH, 0)
    shape = (1,) * nl + (nb // GROUP_BATCH, lt, GROUP_BATCH * HEAD_PAIRS, LANES)
    return shape, lambda i: (i, 0, 0, 0)


def _tile_geometry(tm, seq_len):
    lt = min(tm, seq_len)
    return tm // lt, lt, seq_len // lt


def _bmm_kernel(nb, lt, x_ref, w_ref, o_ref):
    _store_scan_rows(o_ref, (0,), _dot(x_ref[0], w_ref[0]), nb, lt)


def _bmm(x, w, layer, tm, B, L):
    n3, T, D = x.shape
    N = w.shape[3]
    nb, lt, tps = _tile_geometry(tm, L)
    shape, idx = _scan_rows_spec(nb, lt, tps, lead=(0,))
    return pl.pallas_call(
        functools.partial(_bmm_kernel, nb, lt),
        grid=(n3, T // tm),
        in_specs=[pl.BlockSpec((1, tm, D), lambda b, i: (b, i, 0)),
                  pl.BlockSpec((None, 1, D, N), lambda b, i: (layer, b, 0, 0))],
        out_specs=pl.BlockSpec(shape, lambda b, i: (b,) + idx(i)),
        out_shape=jax.ShapeDtypeStruct((n3, B // GROUP_BATCH, L, GROUP_BATCH * HEAD_PAIRS, LANES), F32),
        compiler_params=_cparams("arbitrary", "arbitrary"),
        name="rwkv_rkv_proj",
    )(x, w)


def _lora_kernel(mid, out, geom, x_ref, w1_ref, w2_ref, b_ref, o_ref):
    for rows in _row_chunks(x_ref.shape[0]):
        t = _dot(x_ref[rows, :], w1_ref[...])
        if mid == "tanh":
            t = jnp.tanh(t)
        elif mid == "sigmoid":
            t = _sigmoid(t)
        z = _dot(t.astype(BF16), w2_ref[...]) + b_ref[...]
        if out == "decay":
            w_log = jnp.minimum(z, 0.0) - jnp.log(1.0 + jnp.exp(-jnp.abs(z))) - 0.5
            z = jnp.exp(-jnp.exp(w_log))
        elif out == "sigmoid":
            z = _sigmoid(z)
        if geom is None:
            o_ref[rows, :] = z
        else:
            _store_scan_rows(o_ref, (), z, *geom, row0=rows.start)


def _lora(x, w1, w2, b, mid, out, tm, head_pairs=None):
    T, D = x.shape
    R = w1.shape[1]
    if head_pairs is None:
        geom = None
        out_spec = pl.BlockSpec((tm, D), lambda i: (i, 0))
        out_shape = jax.ShapeDtypeStruct((T, D), F32)
    else:
        B, L = head_pairs
        nb, lt, tps = _tile_geometry(tm, L)
        geom = (nb, lt)
        out_spec = pl.BlockSpec(*_scan_rows_spec(nb, lt, tps))
        out_shape = jax.ShapeDtypeStruct((B // GROUP_BATCH, L, GROUP_BATCH * HEAD_PAIRS, LANES), F32)
    return pl.pallas_call(
        functools.partial(_lora_kernel, mid, out, geom),
        grid=(T // tm,),
        in_specs=[pl.BlockSpec((tm, D), lambda i: (i, 0)),
                  pl.BlockSpec((D, R), lambda i: (0, 0)),
                  pl.BlockSpec((R, D), lambda i: (0, 0)),
                  pl.BlockSpec((1, D), lambda i: (0, 0))],
        out_specs=out_spec,
        out_shape=out_shape,
        compiler_params=_cparams("arbitrary"),
        name="rwkv_lora_" + out,
    )(x, w1, w2, b)


def _wkv_scan_kernel(tb, aliased, r_ref, k_ref, v_ref, w_ref, a_ref, s0_ref, kk_c, ka_c, rk_c, gg_c, gb_c, *rest):
    z_ref, st_ref, s_ref, tile_ref, vec_ref, zbuf_ref, slab_ref = rest[1:] if aliased else rest
    n = RW_HEAD_DIM
    half = LANES // 2
    step0 = pl.program_id(1)
    seq_refs = (r_ref, w_ref, k_ref, v_ref, a_ref)
    R, W, K, V, A = range(5)

    nchunk = n * n // LANES
    per_parity = LANES // 2 // GROUP_BATCH
    chain_rows = [pl.ds(b * 2 * per_parity + hl, per_parity, stride=2) for hl in range(2) for b in range(GROUP_BATCH)]

    @pl.when(step0 == 0)
    def _():
        zbuf_ref[...] = jnp.zeros_like(zbuf_ref)
        x2 = s0_ref[...].reshape(LANES, n * n)
        for c in range(nchunk):
            slab_ref[c] = x2[:, c * LANES:(c + 1) * LANES]
        for c in range(nchunk):
            xt = jnp.concatenate([slab_ref[c, rows, :] for rows in chain_rows], axis=0).T
            for il in range(LANES // n):
                s_ref[pl.ds(c * (LANES // n) + il, n, stride=n), :] = xt[il * n:(il + 1) * n]

    def fetch(t, slot):
        for q, ref in enumerate(seq_refs):
            xt = ref[(0,) * (len(ref.shape) - 3) + (t,)].T
            tile_ref[slot, q] = jnp.concatenate([xt[:n], xt[n:]], axis=1)

    def flush(t, slot):
        z = zbuf_ref[slot]
        z_ref[0, t] = jnp.concatenate([z[:, :half], z[:, half:]], axis=0).T

    def recur(slot):
        k_t = tile_ref[slot, K]
        a_t = tile_ref[slot, A]
        v_t = tile_ref[slot, V]
        kk = k_t * kk_c[...]
        nrm = jnp.sqrt(jnp.sum(kk * kk, axis=0, keepdims=True))
        kk = kk / jnp.maximum(nrm, 1e-12)
        kmod = k_t * (1.0 + (a_t - 1.0) * ka_c[...])
        vec_ref[slot, 0] = kk
        vec_ref[slot, 1] = kk * a_t
        vec_ref[slot, 2] = kmod

        sa = jnp.zeros((n, LANES), F32)
        for j in range(n):
            sa = sa + s_ref[j * n:(j + 1) * n, :] * vec_ref[slot, 0, j:j + 1, :]
        y = jnp.zeros((n, LANES), F32)
        for j in range(n):
            s_new = (s_ref[j * n:(j + 1) * n, :] * tile_ref[slot, W, j:j + 1, :]
                     - sa * vec_ref[slot, 1, j:j + 1, :] + v_t * vec_ref[slot, 2, j:j + 1, :])
            s_ref[j * n:(j + 1) * n, :] = s_new
            y = y + s_new * tile_ref[slot, R, j:j + 1, :]

        mu = jnp.mean(y, axis=0, keepdims=True)
        yc = y - mu
        var = jnp.mean(yc * yc, axis=0, keepdims=True)
        yn = yc * lax.rsqrt(var + RW_GN_EPS) * gg_c[...] + gb_c[...]
        bonus = jnp.sum(tile_ref[slot, R] * kmod * rk_c[...], axis=0, keepdims=True) * v_t
        zbuf_ref[slot] = yn + bonus

    fetch(0, 0)

    def pair(u, carry):
        t = 2 * u
        fetch(t + 1, 1)
        flush(jnp.maximum(t - 1, 0), 1)
        recur(0)
        fetch(jnp.minimum(t + 2, tb - 1), 0)
        flush(t, 0)
        recur(1)
        return carry

    lax.fori_loop(0, tb // 2, pair, 0)
    flush(tb - 1, 1)

    @pl.when(step0 == pl.num_programs(1) - 1)
    def _():
        for c in range(nchunk):
            xt = jnp.concatenate([s_ref[pl.ds(c * (LANES // n) + il, n, stride=n), :]
                                  for il in range(LANES // n)], axis=0).T
            for q, rows in enumerate(chain_rows):
                slab_ref[c, rows, :] = xt[q * per_parity:(q + 1) * per_parity]
        x2 = jnp.concatenate([slab_ref[c] for c in range(nchunk)], axis=1)
        st_ref[...] = x2.reshape(st_ref.shape)


def _wkv_scan(rkv, w, a, state, layer, consts, tb, stacked=None):
    _, G, L, half, _ = rkv.shape
    n = RW_HEAD_DIM
    H = state.shape[2]
    assert tb % 2 == 0 and L % tb == 0 and half == LANES // 2 and GROUP_BATCH * H == LANES
    seq3 = lambda m: pl.BlockSpec((1, 1, tb, half, LANES), lambda g, t, m=m: (m, g, t, 0, 0))
    seq = pl.BlockSpec((1, tb, half, LANES), lambda g, t: (g, t, 0, 0))
    st = pl.BlockSpec((None, GROUP_BATCH, H, n, n), lambda g, t: (layer, g, 0, 0, 0))
    cst = pl.BlockSpec((n, LANES), lambda g, t: (0, 0))
    extra, extra_specs, aliases = (), [], {}
    if stacked is not None:
        extra, extra_specs, aliases = (stacked,), [pl.BlockSpec(memory_space=pl.ANY)], {11: 1}
    return pl.pallas_call(
        functools.partial(_wkv_scan_kernel, tb, stacked is not None),
        grid=(G, L // tb),
        in_specs=[seq3(0), seq3(1), seq3(2), seq, seq, st, cst, cst, cst, cst, cst] + extra_specs,
        out_specs=[seq, st],
        out_shape=[jax.ShapeDtypeStruct((G, L, half, LANES), F32),
                   jax.ShapeDtypeStruct(state.shape, F32)],
        scratch_shapes=[pltpu.VMEM((n * n, LANES), F32), pltpu.VMEM((2, 5, n, LANES), F32),
                        pltpu.VMEM((2, 3, n, LANES), F32), pltpu.VMEM((2, n, LANES), F32),
                        pltpu.VMEM((n * n // LANES, LANES, LANES), F32)],
        input_output_aliases=aliases,
        compiler_params=_cparams("arbitrary", "arbitrary"),
        name="rwkv_scan",
    )(rkv, rkv, rkv, w, a, state, *consts, *extra)


def _rwkv_out_kernel(nb, lt, z_ref, g_ref, x_ref, w_ref, o_ref):
    z = _load_scan_rows(z_ref, nb, lt)
    o_ref[...] = x_ref[...] + _dot((z * g_ref[...]).astype(BF16), w_ref[...])


def _rwkv_out(z, g, x, w, layer, tm, L):
    T, D = x.shape
    nb, lt, tps = _tile_geometry(tm, L)
    tok = pl.BlockSpec((tm, D), lambda i: (i, 0))
    return pl.pallas_call(
        functools.partial(_rwkv_out_kernel, nb, lt),
        grid=(T // tm,),
        in_specs=[pl.BlockSpec(*_scan_rows_spec(nb, lt, tps)),
                  tok, tok, pl.BlockSpec((None, D, D), lambda i: (layer, 0, 0))],
        out_specs=tok,
        out_shape=jax.ShapeDtypeStruct((T, D), F32),
        compiler_params=_cparams("arbitrary"),
        name="rwkv_out",
    )(z, g, x, w)


def _moe_route_kernel(x_ref, g_ref, whi_ref, wlo_ref, b_ref, info_ref, cnt_ref, run_ref):
    h = _rms(x_ref[...], g_ref[...])
    h_hi = h.astype(BF16)
    h_lo = (h - h_hi.astype(F32)).astype(BF16)
    nt = (((1,), (1,)), ((), ()))
    logits = (lax.dot_general(whi_ref[...], h_hi, nt, preferred_element_type=F32)
              + lax.dot_general(whi_ref[...], h_lo, nt, preferred_element_type=F32)
              + lax.dot_general(wlo_ref[...], h_hi, nt, preferred_element_type=F32)
              + b_ref[...])
    gl = [logits[k:k + 1, :] for k in range(N_GROUPS)]
    gmax = functools.reduce(jnp.maximum, gl)
    sel, taken = [], jnp.zeros_like(gmax)
    for k in range(N_GROUPS):
        s = jnp.where((gl[k] == gmax) & (taken == 0.0), 1.0, 0.0)
        taken = taken + s
        sel.append(s)
    p_group = 1.0 / functools.reduce(jnp.add, [jnp.exp(x - gmax) for x in gl])

    el = []
    for m in range(EXP_PER_GROUP):
        rows = [logits[SUBLANES + k * EXP_PER_GROUP + m:SUBLANES + k * EXP_PER_GROUP + m + 1, :]
                for k in range(N_GROUPS)]
        el.append(functools.reduce(jnp.add, [jnp.where(sel[k] > 0.0, rows[k], 0.0) for k in range(N_GROUPS)]))
    emax = functools.reduce(jnp.maximum, el)
    ee = [jnp.exp(x - emax) for x in el]
    esum = functools.reduce(jnp.add, ee)
    prob = [x / esum for x in ee]

    def first_argmax(vals):
        vmax = functools.reduce(jnp.maximum, vals)
        hot, used = [], jnp.zeros_like(vmax)
        for x in vals:
            s = jnp.where((x == vmax) & (used == 0.0), 1.0, 0.0)
            used = used + s
            hot.append(s)
        return hot, vmax

    t1, p1 = first_argmax(prob)
    rest = [jnp.where(t1[m] > 0.0, -1.0, prob[m]) for m in range(EXP_PER_GROUP)]
    t2, p2 = first_argmax(rest)
    scale = p_group / (p1 + p2)
    group = functools.reduce(jnp.add, [sel[k] * float(k * EXP_PER_GROUP) for k in range(N_GROUPS)])
    e1 = group + functools.reduce(jnp.add, [t1[m] * float(m) for m in range(EXP_PER_GROUP)])
    e2 = group + functools.reduce(jnp.add, [t2[m] * float(m) for m in range(EXP_PER_GROUP)])

    @pl.when(pl.program_id(0) == 0)
    def _():
        run_ref[...] = jnp.zeros_like(run_ref)

    tm = e1.shape[1]
    hot1 = jnp.concatenate([sel[k] * t1[m] for k in range(N_GROUPS) for m in range(EXP_PER_GROUP)], axis=0)
    hot2 = jnp.concatenate([sel[k] * t2[m] for k in range(N_GROUPS) for m in range(EXP_PER_GROUP)], axis=0)
    earlier = (lax.broadcasted_iota(jnp.int32, (tm, tm), 0) < lax.broadcasted_iota(jnp.int32, (tm, tm), 1))
    earlier = jnp.where(earlier, 1.0, 0.0).astype(BF16)
    pre1 = _dot(hot1.astype(BF16), earlier)
    pre2 = _dot(hot2.astype(BF16), earlier)
    tot1 = jnp.sum(hot1, axis=1, keepdims=True)
    tot2 = jnp.sum(hot2, axis=1, keepdims=True)
    base = run_ref[:, 0:1]
    rank1 = jnp.sum(hot1 * (base + pre1), axis=0, keepdims=True)
    rank2 = jnp.sum(hot2 * (base + tot1 + pre2), axis=0, keepdims=True)
    run_ref[...] = run_ref[...] + (tot1 + tot2)
    cnt_ref[...] = run_ref[...]

    info_ref[...] = jnp.zeros_like(info_ref)
    info_ref[0:1, :] = e1
    info_ref[1:2, :] = e2
    info_ref[2:3, :] = p1 * scale
    info_ref[3:4, :] = p2 * scale
    info_ref[4:5, :] = rank1
    info_ref[5:6, :] = rank2


def _moe_route(x, g, w_hi, w_lo, b, tm):
    T, D = x.shape
    R = w_hi.shape[0]
    return pl.pallas_call(
        _moe_route_kernel,
        grid=(T // tm,),
        in_specs=[pl.BlockSpec((tm, D), lambda i: (i, 0)),
                  pl.BlockSpec((1, D), lambda i: (0, 0)),
                  pl.BlockSpec((R, D), lambda i: (0, 0)),
                  pl.BlockSpec((R, D), lambda i: (0, 0)),
                  pl.BlockSpec((R, 1), lambda i: (0, 0))],
        out_specs=[pl.BlockSpec((SUBLANES, tm), lambda i: (0, i)),
                   pl.BlockSpec((N_EXPERTS, LANES), lambda i: (0, 0))],
        out_shape=[jax.ShapeDtypeStruct((SUBLANES, T), F32),
                   jax.ShapeDtypeStruct((N_EXPERTS, LANES), F32)],
        scratch_shapes=[pltpu.VMEM((N_EXPERTS, LANES), F32)],
        compiler_params=_cparams("arbitrary"),
        name="moe_route",
    )(x, g, w_hi, w_lo, b)


def _moe_plan(info, counts, tr):
    T = info.shape[1]
    P = 2 * T
    n_tiles = P // tr + N_EXPERTS
    experts = jnp.arange(N_EXPERTS, dtype=jnp.int32)
    eid = info[0:2].astype(jnp.int32).reshape(P)
    rank = info[4:6].astype(jnp.int32).reshape(P)
    tok = jnp.tile(jnp.arange(T, dtype=jnp.int32), 2)
    counts = counts[:, 0].astype(jnp.int32)
    padded = ((counts + tr - 1) // tr) * tr
    ends = jnp.cumsum(padded)
    starts = ends - padded
    slot = jnp.sum(jnp.where(eid[:, None] == experts[None, :], starts[None, :], 0), axis=1) + rank
    src_tok = jnp.zeros((n_tiles * tr,), jnp.int32).at[slot].set(tok)
    n_used = ends[-1] // tr
    tile_row = jnp.minimum(jnp.arange(n_tiles, dtype=jnp.int32), n_used - 1) * tr
    tile_expert = jnp.sum((tile_row[:, None] >= ends[None, :]).astype(jnp.int32), axis=1)
    return src_tok, tile_expert, n_used.reshape(1), slot


def _moe_expert_kernel(tr, src_ref, texp_ref, nused_ref, x_hbm, g_ref, wg_ref, wu_ref, wd_ref, o_ref,
                       xbuf, wg_b, wu_b, wd_b, sem):
    i = pl.program_id(0)
    n_used = nused_ref[0]

    def start_gather(tile, buf):
        def body(r8, c):
            for k in range(SUBLANES):
                r = r8 * SUBLANES + k
                tok = src_ref[tile * tr + r]
                pltpu.make_async_copy(x_hbm.at[pl.ds(tok, 1)], xbuf.at[buf, pl.ds(r, 1)],
                                      sem.at[buf]).start(priority=k % 2)
            return c
        lax.fori_loop(0, tr // SUBLANES, body, 0)

    def wait_gather(buf):
        pltpu.make_async_copy(x_hbm.at[pl.ds(0, tr)], xbuf.at[buf], sem.at[buf]).wait()

    @pl.when(i == 0)
    def _():
        start_gather(0, 0)

    @pl.when(i + 1 < n_used)
    def _():
        start_gather(i + 1, (i + 1) % 2)

    @pl.when((i == 0) | (texp_ref[i] != texp_ref[jnp.maximum(i - 1, 0)]))
    def _():
        wg_b[...] = wg_ref[0].astype(BF16)
        wu_b[...] = wu_ref[0].astype(BF16)
        wd_b[...] = wd_ref[0].astype(BF16)

    @pl.when(i < n_used)
    def _():
        buf = i % 2
        wait_gather(buf)
        h = _rms(xbuf[buf], g_ref[...]).astype(BF16)
        hg = _dot(h, wg_b[...])
        hu = _dot(h, wu_b[...])
        act = hg * _sigmoid(hg) * hu
        o_ref[...] = _dot(act.astype(BF16), wd_b[...])

    @pl.when(i >= n_used)
    def _():
        o_ref[...] = jnp.zeros_like(o_ref)


def _moe_experts(x, g, src_tok, tile_expert, n_used, w_gate, w_up, w_down, layer, tr):
    T, D = x.shape
    Fd = w_gate.shape[3]
    n_tiles = src_tok.shape[0] // tr
    grid_spec = pltpu.PrefetchScalarGridSpec(
        num_scalar_prefetch=3,
        grid=(n_tiles,),
        in_specs=[pl.BlockSpec(memory_space=pl.ANY),
                  pl.BlockSpec((1, D), lambda i, s, e, n: (0, 0)),
                  pl.BlockSpec((None, 1, D, Fd), lambda i, s, e, n: (layer, e[i], 0, 0)),
                  pl.BlockSpec((None, 1, D, Fd), lambda i, s, e, n: (layer, e[i], 0, 0)),
                  pl.BlockSpec((None, 1, Fd, D), lambda i, s, e, n: (layer, e[i], 0, 0))],
        out_specs=pl.BlockSpec((tr, D), lambda i, s, e, n: (i, 0)),
        scratch_shapes=[pltpu.VMEM((2, tr, D), F32), pltpu.VMEM((D, Fd), BF16), pltpu.VMEM((D, Fd), BF16),
                        pltpu.VMEM((Fd, D), BF16), pltpu.SemaphoreType.DMA((2,))],
    )
    return pl.pallas_call(
        functools.partial(_moe_expert_kernel, tr),
        grid_spec=grid_spec,
        out_shape=jax.ShapeDtypeStruct((n_tiles * tr, D), F32),
        compiler_params=_cparams("arbitrary"),
        name="moe_experts",
    )(src_tok, tile_expert, n_used, x, g, w_gate, w_up, w_down)


def _moe_combine_kernel(tm, slot_ref, info_ref, x_ref, ys_hbm, o_ref, ybuf, sem):
    i = pl.program_id(0)
    n = pl.num_programs(0)
    T = n * tm

    def start_gather(tile, buf):
        def body(r4, c):
            for q in range(4):
                r = r4 * 4 + q
                for k in range(2):
                    s = slot_ref[k * T + tile * tm + r]
                    pltpu.make_async_copy(ys_hbm.at[pl.ds(s, 1)], ybuf.at[buf, k, pl.ds(r, 1)],
                                          sem.at[buf]).start(priority=k)
            return c
        lax.fori_loop(0, tm // 4, body, 0)

    def wait_gather(buf):
        for k in range(2):
            pltpu.make_async_copy(ys_hbm.at[pl.ds(0, tm)], ybuf.at[buf, k], sem.at[buf]).wait()

    @pl.when(i == 0)
    def _():
        start_gather(0, 0)

    @pl.when(i + 1 < n)
    def _():
        start_gather(i + 1, (i + 1) % 2)

    buf = i % 2
    wait_gather(buf)
    gates = info_ref[...].T
    o_ref[...] = x_ref[...] + gates[:, 2:3] * ybuf[buf, 0] + gates[:, 3:4] * ybuf[buf, 1]


def _moe_combine(x, info, slot, ys, tm):
    T, D = x.shape
    grid_spec = pltpu.PrefetchScalarGridSpec(
        num_scalar_prefetch=1,
        grid=(T // tm,),
        in_specs=[pl.BlockSpec((SUBLANES, tm), lambda i, s: (0, i)),
                  pl.BlockSpec((tm, D), lambda i, s: (i, 0)),
                  pl.BlockSpec(memory_space=pl.ANY)],
        out_specs=pl.BlockSpec((tm, D), lambda i, s: (i, 0)),
        scratch_shapes=[pltpu.VMEM((2, 2, tm, D), F32), pltpu.SemaphoreType.DMA((2,))],
    )
    return pl.pallas_call(
        functools.partial(_moe_combine_kernel, tm),
        grid_spec=grid_spec,
        out_shape=jax.ShapeDtypeStruct((T, D), F32),
        compiler_params=_cparams("arbitrary"),
        name="moe_combine",
    )(slot, info, x, ys)


def _ple_kernel(x_ref, p_ref, g_ref, wg_ref, wp_ref, o_ref):
    for rows in _row_chunks(x_ref.shape[0]):
        x = x_ref[rows, :]
        gate = _sigmoid(_dot(_rms(x, g_ref[...]).astype(BF16), wg_ref[...]))
        o_ref[rows, :] = x + gate * _dot(p_ref[rows, :].astype(BF16), wp_ref[...])


def _ple(x, p, g, w_gate, w_proj, layer, tm):
    T, D = x.shape
    P = p.shape[2]
    return pl.pallas_call(
        _ple_kernel,
        grid=(T // tm,),
        in_specs=[pl.BlockSpec((tm, D), lambda i: (i, 0)),
                  pl.BlockSpec((None, tm, P), lambda i: (layer, i, 0)),
                  pl.BlockSpec((1, D), lambda i: (0, 0)),
                  pl.BlockSpec((None, D, D), lambda i: (layer, 0, 0)),
                  pl.BlockSpec((None, P, D), lambda i: (layer, 0, 0))],
        out_specs=pl.BlockSpec((tm, D), lambda i: (i, 0)),
        out_shape=jax.ShapeDtypeStruct((T, D), F32),
        compiler_params=_cparams("arbitrary"),
        name="ple",
    )(x, p, g, w_gate, w_proj)


def _final_norm_kernel(x_ref, g_ref, o_ref):
    o_ref[...] = _rms(x_ref[...], g_ref[...])


def _final_norm(x, g, tm):
    T, D = x.shape
    return pl.pallas_call(
        _final_norm_kernel,
        grid=(T // tm,),
        in_specs=[pl.BlockSpec((tm, D), lambda i: (i, 0)), pl.BlockSpec((1, D), lambda i: (0, 0))],
        out_specs=pl.BlockSpec((tm, D), lambda i: (i, 0)),
        out_shape=jax.ShapeDtypeStruct((T, D), F32),
        compiler_params=_cparams("arbitrary"),
        name="final_norm",
    )(x, g)


def _chain_tile(p, H):
    t = p.reshape(H // 2, 2, RW_HEAD_DIM).transpose(2, 1, 0)
    bg = LANES // H
    return jnp.broadcast_to(t[:, :, None, :], (RW_HEAD_DIM, 2, bg, H // 2)).reshape(RW_HEAD_DIM, LANES)


def _trunk(x, p, conv0, shift0, wkv0, W, prompt):
    B, L, D = x.shape
    depth = p.shape[0]
    T = B * L
    d_a = W["vn_g"].shape[1]
    H = D // RW_HEAD_DIM
    lc = min(L, CHUNK)
    x = x.reshape(T, D)
    p = p.reshape(depth, T, -1)
    tm = 512
    tm_mix = 256
    tr = 256 if T >= 4096 else 128
    chunk_v, conv_new, shift_new, wkv_stack = [], [], [], None
    for i in range(depth):
        j = i // 2
        if i % 2 == 0:
            proj = _even_inproj(x, W["g_mix"][i][None], W["w_in"], j, d_a, min(T, 2 * tm), d_a)
            ws_eff = jnp.tile(W["w_s"][j][:, :lc, :lc], (1, CHUNK // lc, CHUNK // lc))
            bias = jnp.tile(jnp.repeat(W["b_s"][j][:, :lc].T, A_GROUP_DIM, axis=1), (CHUNK // lc, 1))
            st = conv0[j]
            if prompt:
                b1 = b2 = st
            else:
                zero = jnp.zeros((B, L, st.shape[-1]), F32)
                b1 = zero.at[:, 0].set(st[:, 1]).reshape(T, -1)
                b2 = zero.at[:, 0].set(st[:, 0]).at[:, 1].set(st[:, 1]).reshape(T, -1)
            x, v_all, cx_all = _even_mix(proj, x, W["vn_g"][j][None], W["vn_b"][j][None], ws_eff, bias,
                                         W["conv_w"][j], W["w_out"], j, b1, b2, prompt, L, tm_mix, d_a)
            start = ((L - 1) // CHUNK) * CHUNK
            chunk_v.append(v_all.reshape(B, L, -1)[:, start:])
            conv_new.append(cx_all.reshape(B, L, -1)[:, L - 2:])
        else:
            st = shift0[j]
            if prompt:
                bnd = st[:, None, :]
            else:
                bnd = jnp.zeros((B, L, D), F32).at[:, 0].set(st).reshape(T, D)
            xrkv, xw, xa, xg, h_all = _rwkv_prep(x, W["g_mix"][i][None], W["rw_mu"][j], bnd, prompt, L, tm_mix)
            rkv = _bmm(xrkv, W["rw_w_rkv"], j, tm, B, L)
            decay = _lora(xw, W["rw_w1"][j], W["rw_w2"][j], W["rw_w0"][j][None], "tanh", "decay", tm, (B, L))
            a = _lora(xa, W["rw_a1"][j], W["rw_a2"][j], W["rw_a0"][j][None], "none", "sigmoid", tm, (B, L))
            g = _lora(xg, W["rw_g1"][j], W["rw_g2"][j], jnp.zeros((1, D), F32), "sigmoid", "none", tm)
            consts = (_chain_tile(W["rw_k_k"][j], H), _chain_tile(W["rw_k_a"][j], H),
                      _chain_tile(W["rw_r_k"][j].reshape(-1), H),
                      _chain_tile(W["rw_gn_g"][j], H), _chain_tile(W["rw_gn_b"][j], H))
            z, wkv_stack = _wkv_scan(rkv, decay, a, wkv0, j, consts, min(L, 32), wkv_stack)
            x = _rwkv_out(z, g, x, W["rw_w_o"], j, tm_mix, L)
            shift_new.append(h_all.reshape(B, L, D)[:, -1])
        info, counts = _moe_route(x, W["g_ffn"][i][None], W["moe_r_hi"][i], W["moe_r_lo"][i], W["moe_r_b"][i], tm)
        src_tok, tile_expert, n_used, slot = _moe_plan(info, counts, tr)
        ys = _moe_experts(x, W["g_ffn"][i][None], src_tok, tile_expert, n_used,
                          W["moe_w_gate"], W["moe_w_up"], W["moe_w_down"], i, tr)
        x = _moe_combine(x, info, slot, ys, tm_mix)
        x = _ple(x, p, W["ple_g"][i][None], W["ple_w_gate"], W["ple_w_proj"], i, tm)
    y = _final_norm(x, W["g_final"][None], tm).reshape(B, L, D)
    return y, jnp.stack(chunk_v), jnp.stack(conv_new), jnp.stack(shift_new), wkv_stack


def _pad_rank(w1, w2):
    r = w1.shape[-1]
    rp = -(-r // LANES) * LANES
    w1 = jnp.pad(w1, ((0, 0), (0, 0), (0, rp - r)))
    w2 = jnp.pad(w2, ((0, 0), (0, rp - r), (0, 0)))
    return w1.astype(BF16), w2.astype(BF16)


def kernel(x_prompt, x_sample, state_conv, state_shift, state_wkv, p_prompt, p_sample, g_mix, g_ffn, g_final, w_in_even, vn_g, vn_b, w_s, b_s, conv_w, w_out_even, rw_mu, rw_w_rkv, rw_w0, rw_w1, rw_w2, rw_a0, rw_a1, rw_a2, rw_g1, rw_g2, rw_k_k, rw_k_a, rw_r_k, rw_gn_g, rw_gn_b, rw_w_o, moe_w_gr, moe_b_gr, moe_w_er, moe_b_er, moe_w_gate, moe_w_up, moe_w_down, ple_g, ple_w_gate, ple_w_proj):
    depth, D = g_mix.shape
    r_w = jnp.zeros((depth, 32, D), F32)
    r_w = r_w.at[:, :N_GROUPS].set(jnp.swapaxes(moe_w_gr, 1, 2))
    r_w = r_w.at[:, SUBLANES:SUBLANES + N_EXPERTS].set(jnp.swapaxes(moe_w_er, 1, 2))
    r_hi = r_w.astype(BF16)
    r_lo = (r_w - r_hi.astype(F32)).astype(BF16)
    r_b = jnp.zeros((depth, 32, 1), F32)
    r_b = r_b.at[:, :N_GROUPS, 0].set(moe_b_gr).at[:, SUBLANES:SUBLANES + N_EXPERTS, 0].set(moe_b_er)
    w1, w2 = _pad_rank(rw_w1, rw_w2)
    a1, a2 = _pad_rank(rw_a1, rw_a2)
    g1, g2 = _pad_rank(rw_g1, rw_g2)
    W = dict(g_mix=g_mix, g_ffn=g_ffn, g_final=g_final,
             w_in=w_in_even.astype(BF16), vn_g=vn_g, vn_b=vn_b, w_s=w_s, b_s=b_s, conv_w=conv_w,
             w_out=w_out_even.astype(BF16),
             rw_mu=rw_mu, rw_w_rkv=rw_w_rkv.astype(BF16), rw_w0=rw_w0, rw_w1=w1, rw_w2=w2,
             rw_a0=rw_a0, rw_a1=a1, rw_a2=a2, rw_g1=g1, rw_g2=g2,
             rw_k_k=rw_k_k, rw_k_a=rw_k_a, rw_r_k=rw_r_k, rw_gn_g=rw_gn_g, rw_gn_b=rw_gn_b,
             rw_w_o=rw_w_o.astype(BF16),
             moe_r_hi=r_hi, moe_r_lo=r_lo, moe_r_b=r_b,
             moe_w_gate=moe_w_gate, moe_w_up=moe_w_up, moe_w_down=moe_w_down,
             ple_g=ple_g, ple_w_gate=ple_w_gate.astype(BF16), ple_w_proj=ple_w_proj.astype(BF16))
    bp = x_prompt.shape[0]
    n_even, n_odd = state_conv.shape[0], state_shift.shape[0]
    conv0 = jnp.zeros((n_even, bp) + state_conv.shape[2:], F32)
    shift0 = jnp.zeros((n_odd, bp, D), F32)
    wkv0 = jnp.zeros((n_odd, bp) + state_wkv.shape[2:], F32)
    y_p, cv_p, conv_p, shift_p, wkv_p = _trunk(x_prompt, p_prompt, conv0, shift0, wkv0, W, True)
    y_s, cv_s, conv_s, shift_s, wkv_s = _trunk(x_sample, p_sample, state_conv, state_shift, state_wkv, W, False)
    return (y_p, y_s, cv_p, conv_p, shift_p, wkv_p, cv_s, conv_s, shift_s, wkv_s)
```
